```python
import jax, jax.numpy as jnp
from jax import lax
import numpy as np

D_MODEL = 1024
BATCH = 16
SEQ = 4096
DEPTH = 4

HEAD_DIM = 64
N_Q_A = 16
N_KV_A = 2
GROUP_A = N_Q_A // N_KV_A
WINDOW = 128
BLOCK = 128
N_H_B = 16
ROT_DIM = HEAD_DIM // 4
ROPE_THETA = 500000.0
D_FF = -(-8 * D_MODEL // (3 * 256)) * 256
N_MIXERS = 2
N_A = (DEPTH + 1) // 2
N_B = DEPTH // 2
QKV_A = (N_Q_A + 2 * N_KV_A) * HEAD_DIM
QKV_B = 3 * N_H_B * HEAD_DIM
EPS = 1e-6

kernel_name = 'hybrid_swa_sink_stickbreak_block'


def rmsnorm(x, gain):
    xf = x.astype(jnp.float32)
    y = xf * lax.rsqrt(jnp.mean(xf * xf, axis=-1, keepdims=True) + EPS)
    return (y * gain.astype(jnp.float32)).astype(x.dtype)


def partial_rope(x, positions):
    half = ROT_DIM // 2
    inv_freq = jnp.power(jnp.float32(ROPE_THETA), -jnp.arange(half, dtype=jnp.float32) * 2.0 / ROT_DIM)
    ang = positions.astype(jnp.float32)[:, :, None, None] * inv_freq
    cos, sin = jnp.cos(ang), jnp.sin(ang)
    xr = x[..., :ROT_DIM].astype(jnp.float32)
    x1, x2 = xr[..., :half], xr[..., half:]
    rot = jnp.concatenate([x1 * cos - x2 * sin, x2 * cos + x1 * sin], axis=-1).astype(x.dtype)
    return jnp.concatenate([rot, x[..., ROT_DIM:]], axis=-1)


def sliding_window_sink_attention(h, positions, w_qkv, q_gain, k_gain, sinks, w_o):
    B, S, _ = h.shape
    qkv = h @ w_qkv
    q, k, v = jnp.split(qkv, [N_Q_A * HEAD_DIM, (N_Q_A + N_KV_A) * HEAD_DIM], axis=-1)
    q = q.reshape(B, S, N_Q_A, HEAD_DIM)
    k = k.reshape(B, S, N_KV_A, HEAD_DIM)
    v = v.reshape(B, S, N_KV_A, HEAD_DIM)
    q = partial_rope(rmsnorm(q, q_gain), positions)
    k = partial_rope(rmsnorm(k, k_gain), positions)
    q = q.reshape(B, S, N_KV_A, GROUP_A, HEAD_DIM)
    pad = jnp.zeros((B, BLOCK, N_KV_A, HEAD_DIM), k.dtype)
    kp = jnp.concatenate([pad, k], axis=1)
    vp = jnp.concatenate([pad, v], axis=1)
    scale = HEAD_DIM ** -0.5
    q_idx = jnp.arange(BLOCK)[:, None] + BLOCK
    k_idx = jnp.arange(2 * BLOCK)[None, :]
    rel = q_idx - k_idx
    band = (rel >= 0) & (rel < WINDOW)
    sink_logit = sinks.astype(jnp.float32).reshape(1, N_KV_A, GROUP_A, 1, 1)

    def block_fn(i):
        start = i * BLOCK
        qb = lax.dynamic_slice_in_dim(q, start, BLOCK, axis=1)
        kb = lax.dynamic_slice_in_dim(kp, start, 2 * BLOCK, axis=1)
        vb = lax.dynamic_slice_in_dim(vp, start, 2 * BLOCK, axis=1)
        s = jnp.einsum('bqkgd,bskd->bkgqs', qb, kb).astype(jnp.float32) * scale
        valid = band & (start - BLOCK + k_idx >= 0)
        s = jnp.where(valid, s, -jnp.inf)
        sink_col = jnp.broadcast_to(sink_logit, s.shape[:-1] + (1,))
        p = jax.nn.softmax(jnp.concatenate([s, sink_col], axis=-1), axis=-1)[..., :-1]
        o = jnp.einsum('bkgqs,bskd->bqkgd', p.astype(vb.dtype), vb)
        return o.reshape(B, BLOCK, N_Q_A * HEAD_DIM)

    out = lax.map(block_fn, jnp.arange(S // BLOCK))
    out = jnp.moveaxis(out, 0, 1).reshape(B, S, N_Q_A * HEAD_DIM)
    return out @ w_o


def stick_breaking_attention(h, w_qkv, w_o):
    B, S, _ = h.shape
    qkv = h @ w_qkv
    q, k, v = jnp.split(qkv, 3, axis=-1)
    q = q.reshape(B, S, N_H_B, HEAD_DIM)
    k = k.reshape(B, S, N_H_B, HEAD_DIM)
    v = v.reshape(B, S, N_H_B, HEAD_DIM)
    scale = HEAD_DIM ** -0.5
    k_idx = jnp.arange(S)[None, :]

    def block_fn(i):
        start = i * BLOCK
        qb = lax.dynamic_slice_in_dim(q, start, BLOCK, axis=1)
        z = jnp.einsum('bqhd,bshd->bhqs', qb, k).astype(jnp.float32) * scale
        t_idx = start + jnp.arange(BLOCK)[:, None]
        strict = k_idx < t_idx
        log_beta = jax.nn.log_sigmoid(z)
        log_one_minus = jnp.where(strict, jax.nn.log_sigmoid(-z), 0.0)
        rc = lax.cumsum(log_one_minus, axis=3, reverse=True)
        after = jnp.pad(rc[..., 1:], ((0, 0), (0, 0), (0, 0), (0, 1)))
        a = jnp.where(strict, jnp.exp(log_beta + after), 0.0)
        o = jnp.einsum('bhqs,bshd->bqhd', a.astype(v.dtype), v)
        return o.reshape(B, BLOCK, N_H_B * HEAD_DIM)

    out = lax.map(block_fn, jnp.arange(S // BLOCK))
    out = jnp.moveaxis(out, 0, 1).reshape(B, S, N_H_B * HEAD_DIM)
    return out @ w_o


def swiglu(h, w_gate, w_up, w_down):
    return (jax.nn.silu(h @ w_gate) * (h @ w_up)) @ w_down


def setup_inputs(seed: int = 0) -> dict:
    key = jax.random.key(seed)
    ks = jax.random.split(key, 20)
    f32 = jnp.float32
    nrm = lambda k, shape, s: jax.random.normal(k, shape, f32) * s
    x = jax.random.normal(ks[0], (BATCH, SEQ, D_MODEL), f32)
    c = jax.random.normal(ks[1], (BATCH, D_MODEL), f32)
    offset = jax.random.randint(ks[2], (BATCH, 1), 0, 4096, dtype=jnp.int32)
    positions = offset + jnp.arange(SEQ, dtype=jnp.int32)[None, :]
    return {
        'x': x,
        'c': c,
        'positions': positions,
        'ada_w': nrm(ks[3], (DEPTH, D_MODEL, 6 * D_MODEL), 0.5 * D_MODEL ** -0.5),
        'ada_b': nrm(ks[4], (DEPTH, 6 * D_MODEL), 0.01),
        'norm1_g': 1.0 + nrm(ks[5], (DEPTH, D_MODEL), 0.05),
        'norm2_g': 1.0 + nrm(ks[6], (DEPTH, D_MODEL), 0.05),
        'wqkv_a': nrm(ks[7], (N_A, D_MODEL, QKV_A), D_MODEL ** -0.5),
        'q_norm_a': 1.0 + nrm(ks[8], (N_A, HEAD_DIM), 0.05),
        'k_norm_a': 1.0 + nrm(ks[9], (N_A, HEAD_DIM), 0.05),
        'sinks_a': nrm(ks[10], (N_A, N_Q_A), 1.0),
        'wo_a': nrm(ks[11], (N_A, N_Q_A * HEAD_DIM, D_MODEL), (N_Q_A * HEAD_DIM) ** -0.5),
        'wqkv_b': nrm(ks[12], (N_B, D_MODEL, QKV_B), D_MODEL ** -0.5),
        'wo_b': nrm(ks[13], (N_B, N_H_B * HEAD_DIM, D_MODEL), (N_H_B * HEAD_DIM) ** -0.5),
        'w_gate': nrm(ks[14], (DEPTH, D_MODEL, D_FF), D_MODEL ** -0.5),
        'w_up': nrm(ks[15], (DEPTH, D_MODEL, D_FF), D_MODEL ** -0.5),
        'w_down': nrm(ks[16], (DEPTH, D_FF, D_MODEL), D_FF ** -0.5),
    }


def reference(x, c, positions, ada_w, ada_b, norm1_g, norm2_g, wqkv_a, q_norm_a, k_norm_a,
              sinks_a, wo_a, wqkv_b, wo_b, w_gate, w_up, w_down):
    cond = jax.nn.silu(c)
    for i in range(DEPTH):
        mod = (cond @ ada_w[i] + ada_b[i])[:, None, :]
        sh1, sc1, g1, sh2, sc2, g2 = jnp.split(mod, 6, axis=-1)
        h = rmsnorm(x, norm1_g[i]) * (1.0 + sc1) + sh1
        j = i // N_MIXERS
        if i % N_MIXERS == 0:
            y = sliding_window_sink_attention(h, positions, wqkv_a[j], q_norm_a[j], k_norm_a[j],
                                              sinks_a[j], wo_a[j])
        else:
            y = stick_breaking_attention(h, wqkv_b[j], wo_b[j])
        x = x + g1 * y
        h = rmsnorm(x, norm2_g[i]) * (1.0 + sc2) + sh2
        x = x + g2 * swiglu(h, w_gate[i], w_up[i], w_down[i])
    return x
```

```python
import functools

import jax
import jax.numpy as jnp
from jax import lax
from jax.experimental import pallas as pl
from jax.experimental.pallas import tpu as pltpu

F32 = jnp.float32
BF16 = jnp.bfloat16

HEAD_DIM = 64
N_Q_A = 16
N_KV_A = 2
GROUP_A = N_Q_A // N_KV_A
N_H_B = 16
BLOCK = 128
WINDOW = 128
ROT_DIM = HEAD_DIM // 4
ROPE_THETA = 500000.0
EPS = 1e-6
LANES = 128
FF_CHUNK = 256
ROW_TILE = 512
VMEM_LIMIT = 56 * 1024 * 1024
SB_CUTOFF = -104.0

_NT = (((1,), (1,)), ((), ()))


def _params(sem):
    return pltpu.CompilerParams(dimension_semantics=sem, vmem_limit_bytes=VMEM_LIMIT)


def _resident(shape):
    return pl.BlockSpec(shape, lambda *_: (0,) * len(shape), pipeline_mode=pl.Buffered(1))


def _rms_mod(x, gain, scale, shift):
    ms = jnp.mean(x * x, axis=-1, keepdims=True)
    y = x * lax.rsqrt(ms + EPS)
    return (y * gain) * (1.0 + scale) + shift


def _split_dot(x, m):
    hi = x.astype(BF16)
    lo = (x - hi.astype(F32)).astype(BF16)
    return (jnp.dot(hi, m, preferred_element_type=F32)
            + jnp.dot(lo, m, preferred_element_type=F32))


def _adaln_kernel(c_ref, w_ref, b_ref, o_ref):
    c = c_ref[...]
    cond = (c * jax.nn.sigmoid(c)).astype(BF16)
    o_ref[0] = jnp.dot(cond, w_ref[0].astype(BF16), preferred_element_type=F32) + b_ref[0]


def _adaln(c, ada_w, ada_b):
    depth, d, n = ada_w.shape
    b = c.shape[0]
    tn = 1024
    return pl.pallas_call(
        _adaln_kernel,
        out_shape=jax.ShapeDtypeStruct((depth, b, n), F32),
        grid=(depth, n // tn),
        in_specs=[
            pl.BlockSpec((b, d), lambda l, j: (0, 0)),
            pl.BlockSpec((1, d, tn), lambda l, j: (l, 0, j)),
            pl.BlockSpec((1, 1, tn), lambda l, j: (l, 0, j)),
        ],
        out_specs=pl.BlockSpec((1, b, tn), lambda l, j: (l, 0, j)),
        compiler_params=_params(("arbitrary", "arbitrary")),
        name="adaln",
    )(c, ada_w, ada_b.reshape(depth, 1, n))


def _rope_kernel(pos_ref, invf_ref, cos_ref, sin_ref, nsin_ref):
    ang = pos_ref[...].astype(F32) * invf_ref[...]
    s = jnp.sin(ang)
    cos_ref[...] = jnp.cos(ang)
    sin_ref[...] = s
    nsin_ref[...] = -s


def _rope_tables(positions):
    b, s = positions.shape
    t = b * s
    half = ROT_DIM // 2
    inv_freq = jnp.power(F32(ROPE_THETA), -jnp.arange(half, dtype=F32) * 2.0 / ROT_DIM)
    tl = min(t, 8192)
    shp = jax.ShapeDtypeStruct((half, t), F32)
    cos, sin, nsin = pl.pallas_call(
        _rope_kernel,
        out_shape=(shp, shp, shp),
        grid=(t // tl,),
        in_specs=[pl.BlockSpec((1, tl), lambda i: (0, i)),
                  pl.BlockSpec((half, 1), lambda i: (0, 0))],
        out_specs=(pl.BlockSpec((half, tl), lambda i: (0, i)),) * 3,
        compiler_params=_params(("arbitrary",)),
        name="rope_tables",
    )(positions.reshape(1, t), inv_freq.reshape(half, 1))
    cos, sin, nsin = cos.T, sin.T, nsin.T
    one = jnp.ones((t, HEAD_DIM - ROT_DIM), F32)
    zero = jnp.zeros((t, HEAD_DIM - ROT_DIM), F32)
    zh = jnp.zeros((t, half), F32)
    cos_full = jnp.concatenate([cos, cos, one] * 2, axis=1)
    sin_up = jnp.concatenate([nsin, zh, zero] * 2, axis=1)
    sin_dn = jnp.concatenate([zh, sin, zero] * 2, axis=1)
    return (cos_full.reshape(b, s, LANES), sin_up.reshape(b, s, LANES),
            sin_dn.reshape(b, s, LANES))


def _proj_a_kernel(x_ref, mod_ref, g_ref, w_ref, gain_ref, gsum_ref, cos_ref, sup_ref, sdn_ref,
                   q_ref, k_ref, v_ref):
    h = _rms_mod(x_ref[0], g_ref[...], mod_ref[0, 1:2, :], mod_ref[0, 0:1, :]).astype(BF16)
    qkv = jnp.dot(h, w_ref[...], preferred_element_type=F32)
    cos, sup, sdn = cos_ref[0], sup_ref[0], sdn_ref[0]
    gsum = gsum_ref[...]
    half = ROT_DIM // 2

    def norm_rope(blk, gain):
        ssq = _split_dot(blk * blk, gsum)
        y = blk * lax.rsqrt(ssq * (1.0 / HEAD_DIM) + EPS) * gain
        return (y * cos + pltpu.roll(y, LANES - half, 1) * sup + pltpu.roll(y, half, 1) * sdn)

    nq = q_ref.shape[-1] // LANES
    for cb in range(nq):
        blk = qkv[:, cb * LANES:(cb + 1) * LANES]
        q_ref[0, :, cb * LANES:(cb + 1) * LANES] = norm_rope(blk, gain_ref[0:1, :]).astype(BF16)
    k_ref[0] = norm_rope(qkv[:, nq * LANES:(nq + 1) * LANES], gain_ref[1:2, :]).astype(BF16)
    v_ref[0] = qkv[:, (nq + 1) * LANES:(nq + 2) * LANES].astype(BF16)


def _proj_a(x, mod, gain1, w, gains, gsum, cos, sup, sdn):
    b, s, d = x.shape
    tm = ROW_TILE
    nq = N_Q_A * HEAD_DIM
    row = lambda n: pl.BlockSpec((1, tm, n), lambda bi, i: (bi, i, 0))
    out = lambda n: jax.ShapeDtypeStruct((b, s, n), BF16)
    return pl.pallas_call(
        _proj_a_kernel,
        out_shape=(out(nq), out(LANES), out(LANES)),
        grid=(b, s // tm),
        in_specs=[
            row(d),
            pl.BlockSpec((1, 6, d), lambda bi, i: (bi, 0, 0)),
            _resident((1, d)),
            _resident(w.shape),
            _resident(gains.shape),
            _resident(gsum.shape),
            row(LANES), row(LANES), row(LANES),
        ],
        out_specs=(row(nq), row(LANES), row(LANES)),
        compiler_params=_params(("parallel", "parallel")),
        name="proj_a",
    )(x, mod, gain1, w, gains, gsum, cos, sup, sdn)


def _attn_a_kernel(sink_ref, q_ref, kc_ref, kp_ref, vc_ref, vp_ref, o_ref):
    i = pl.program_id(1)
    kcat = jnp.concatenate([kp_ref[0], kc_ref[0]], axis=0)
    vcat = jnp.concatenate([vp_ref[0], vc_ref[0]], axis=0)
    r = lax.broadcasted_iota(jnp.int32, (2 * BLOCK, 2 * BLOCK), 0)
    c = lax.broadcasted_iota(jnp.int32, (2 * BLOCK, 2 * BLOCK), 1)
    rel = (r & (BLOCK - 1)) + BLOCK - c
    valid = (rel >= 0) & (rel < WINDOW) & (c + (i - 1) * BLOCK >= 0)
    lane2 = lax.broadcasted_iota(jnp.int32, (2 * BLOCK, LANES), 1)
    row2 = lax.broadcasted_iota(jnp.int32, (2 * BLOCK, LANES), 0)
    keep = (lane2 < HEAD_DIM) == (row2 < BLOCK)
    rowc = lax.broadcasted_iota(jnp.int32, (2 * BLOCK, 1), 0)
    lane1 = lax.broadcasted_iota(jnp.int32, (BLOCK, LANES), 1)
    for p in range(GROUP_A):
        qp = q_ref[0, :, p * LANES:(p + 1) * LANES]
        qq = jnp.concatenate([qp, qp], axis=0)
        lhs = jnp.where(keep, qq, jnp.zeros_like(qq))
        s = lax.dot_general(lhs, kcat, _NT, preferred_element_type=F32)
        s = jnp.where(valid, s, -1e30)
        sink = jnp.where(rowc < BLOCK, sink_ref[p], sink_ref[p + GROUP_A])
        m = jnp.maximum(jnp.max(s, axis=-1, keepdims=True), sink)
        e = jnp.exp(s - m)
        denom = jnp.sum(e, axis=-1, keepdims=True) + jnp.exp(sink - m)
        pv = jnp.dot(e.astype(BF16), vcat, preferred_element_type=F32) * (1.0 / denom)
        o_ref[0, :, p * LANES:(p + 1) * LANES] = jnp.where(
            lane1 < HEAD_DIM, pv[:BLOCK], pv[BLOCK:]).astype(BF16)


def _attn_a(sinks, q, k, v):
    b, s, nq = q.shape
    cur = lambda bi, i: (bi, i, 0)
    prev = lambda bi, i: (bi, jnp.maximum(i - 1, 0), 0)
    return pl.pallas_call(
        _attn_a_kernel,
        out_shape=jax.ShapeDtypeStruct((b, s, nq), BF16),
        grid=(b, s // BLOCK),
        in_specs=[
            pl.BlockSpec(memory_space=pltpu.SMEM),
            pl.BlockSpec((1, BLOCK, nq), cur),
            pl.BlockSpec((1, BLOCK, LANES), cur),
            pl.BlockSpec((1, BLOCK, LANES), prev),
            pl.BlockSpec((1, BLOCK, LANES), cur),
            pl.BlockSpec((1, BLOCK, LANES), prev),
        ],
        out_specs=pl.BlockSpec((1, BLOCK, nq), cur),
        compiler_params=_params(("parallel", "parallel")),
        name="attn_a",
    )(sinks, q, k, k, v, v)


def _proj_b_kernel(x_ref, mod_ref, g_ref, w_ref, o_ref):
    h = _rms_mod(x_ref[0], g_ref[...], mod_ref[0, 1:2, :], mod_ref[0, 0:1, :]).astype(BF16)
    d = w_ref.shape[0]
    for n in range(w_ref.shape[1] // d):
        o_ref[0, :, n * d:(n + 1) * d] = jnp.dot(
            h, w_ref[:, n * d:(n + 1) * d], preferred_element_type=F32).astype(BF16)


def _proj_b(x, mod, gain1, w):
    b, s, d = x.shape
    tm = ROW_TILE
    n = w.shape[1]
    return pl.pallas_call(
        _proj_b_kernel,
        out_shape=jax.ShapeDtypeStruct((b, s, n), BF16),
        grid=(b, s // tm),
        in_specs=[
            pl.BlockSpec((1, tm, d), lambda bi, i: (bi, i, 0)),
            pl.BlockSpec((1, 6, d), lambda bi, i: (bi, 0, 0)),
            _resident((1, d)),
            _resident(w.shape),
        ],
        out_specs=pl.BlockSpec((1, tm, n), lambda bi, i: (bi, i, 0)),
        compiler_params=_params(("parallel", "parallel")),
        name="proj_b",
    )(x, mod, gain1, w)


def _attn_b_kernel(q_ref, k_ref, v_ref, uo_ref, o_ref):
    uo = uo_ref[...]
    lane = lax.broadcasted_iota(jnp.int32, (BLOCK, LANES), 1)
    row = lax.broadcasted_iota(jnp.int32, (BLOCK, BLOCK), 0)
    col = lax.broadcasted_iota(jnp.int32, (BLOCK, BLOCK), 1)
    strict = col < row
    head0 = lane < HEAD_DIM
    zeros = jnp.zeros((BLOCK, BLOCK), F32)

    def tile(qm, j, carry, acc, diag):
        start = pl.multiple_of(j * BLOCK, BLOCK)
        kb = k_ref[0, pl.ds(start, BLOCK), :]
        vb = v_ref[0, pl.ds(start, BLOCK), :]
        z = lax.dot_general(qm, kb, _NT, preferred_element_type=F32)
        log_beta = jnp.minimum(z, 0.0) - jnp.log1p(jnp.exp(-jnp.abs(z)))
        log_rest = log_beta - z
        if diag:
            log_rest = jnp.where(strict, log_rest, 0.0)
        sums = _split_dot(log_rest, uo)
        a = jnp.exp(log_beta + sums[:, :BLOCK] + carry)
        if diag:
            a = jnp.where(strict, a, 0.0)
        acc = acc + jnp.dot(a.astype(BF16), vb, preferred_element_type=F32)
        return carry + sums[:, BLOCK:], acc

    def qblock(i, _):
        qs = pl.multiple_of(i * BLOCK, BLOCK)
        q = q_ref[0, pl.ds(qs, BLOCK), :]
        q0 = jnp.where(head0, q, jnp.zeros_like(q))
        q1 = jnp.where(head0, jnp.zeros_like(q), q)
        c0, a0 = tile(q0, i, zeros, zeros, True)
        c1, a1 = tile(q1, i, zeros, zeros, True)

        def cond(st):
            return (st[0] >= 0) & (st[1] > 0)

        def body(st):
            j, _, c0, a0, c1, a1 = st
            c0, a0 = tile(q0, j, c0, a0, False)
            c1, a1 = tile(q1, j, c1, a1, False)
            live = jnp.maximum(jnp.max(c0), jnp.max(c1)) > SB_CUTOFF
            return j - 1, live.astype(jnp.int32), c0, a0, c1, a1

        st = lax.while_loop(cond, body, (i - 1, jnp.int32(1), c0, a0, c1, a1))
        o_ref[0, pl.ds(qs, BLOCK), :] = jnp.where(head0, st[3], st[5]).astype(BF16)
        return 0

    lax.fori_loop(0, q_ref.shape[1] // BLOCK, qblock, 0)


def _attn_b(qkv, uo):
    b, s, n = qkv.shape
    npair = n // 3 // LANES
    blk = lambda off: pl.BlockSpec((1, s, LANES), lambda bi, hp: (bi, 0, off + hp))
    return pl.pallas_call(
        _attn_b_kernel,
        out_shape=jax.ShapeDtypeStruct((b, s, n // 3), BF16),
        grid=(b, npair),
        in_specs=[blk(0), blk(npair), blk(2 * npair), _resident(uo.shape)],
        out_specs=blk(0),
        compiler_params=_params(("parallel", "parallel")),
        name="attn_b",
    )(qkv, qkv, qkv, uo)


def _post_kernel(o_ref, x_ref, mod_ref, g_ref, wo_ref, wg_ref, wu_ref, wd_ref, out_ref, acc_ref):
    y = jnp.dot(o_ref[0], wo_ref[...], preferred_element_type=F32)
    x1 = x_ref[0] + mod_ref[0, 2:3, :] * y
    h = _rms_mod(x1, g_ref[...], mod_ref[0, 4:5, :], mod_ref[0, 3:4, :]).astype(BF16)
    acc_ref[...] = jnp.zeros_like(acc_ref)

    def chunk(f, _):
        g = jnp.dot(h, wg_ref[f], preferred_element_type=F32)
        u = jnp.dot(h, wu_ref[f], preferred_element_type=F32)
        act = ((g * jax.nn.sigmoid(g)) * u).astype(BF16)
        acc_ref[...] += jnp.dot(act, wd_ref[f], preferred_element_type=F32)
        return 0

    lax.fori_loop(0, wg_ref.shape[0], chunk, 0)
    out_ref[0] = x1 + mod_ref[0, 5:6, :] * acc_ref[...]


def _post(o, x, mod, gain2, wo, wg, wu, wd):
    b, s, d = x.shape
    tm = ROW_TILE
    row = lambda n: pl.BlockSpec((1, tm, n), lambda bi, i: (bi, i, 0))
    return pl.pallas_call(
        _post_kernel,
        out_shape=jax.ShapeDtypeStruct((b, s, d), F32),
        grid=(b, s // tm),
        in_specs=[
            row(o.shape[-1]), row(d),
            pl.BlockSpec((1, 6, d), lambda bi, i: (bi, 0, 0)),
            _resident((1, d)),
            _resident(wo.shape), _resident(wg.shape), _resident(wu.shape), _resident(wd.shape),
        ],
        out_specs=row(d),
        scratch_shapes=[pltpu.VMEM((tm, d), F32)],
        compiler_params=_params(("parallel", "parallel")),
        name="post",
    )(o, x, mod, gain2, wo, wg, wu, wd)


def _pair_perm():
    cols = []
    for p in range(GROUP_A):
        for h in (p, p + GROUP_A):
            cols.extend(range(h * HEAD_DIM, (h + 1) * HEAD_DIM))
    return jnp.asarray(cols, jnp.int32)


def kernel(x, c, positions, ada_w, ada_b, norm1_g, norm2_g, wqkv_a, q_norm_a, k_norm_a, sinks_a,
           wo_a, wqkv_b, wo_b, w_gate, w_up, w_down):
    b, s, d = x.shape
    depth = ada_w.shape[0]
    d_ff = w_gate.shape[-1]
    nf = d_ff // FF_CHUNK
    scale = HEAD_DIM ** -0.5

    mod = _adaln(c, ada_w, ada_b).reshape(depth, b, 6, d)
    cos, sup, sdn = _rope_tables(positions)

    perm = _pair_perm()
    nqa = N_Q_A * HEAD_DIM
    lane = jnp.arange(LANES)
    gsum = (lane[:, None] // HEAD_DIM == lane[None, :] // HEAD_DIM).astype(BF16)
    tri = (lane[:, None] > lane[None, :]).astype(BF16)
    uo = jnp.concatenate([tri, jnp.ones((BLOCK, BLOCK), BF16)], axis=1)

    for i in range(depth):
        j = i // 2
        if i % 2 == 0:
            w = wqkv_a[j]
            w = jnp.concatenate([w[:, :nqa][:, perm], w[:, nqa:]], axis=1).astype(BF16)
            gains = jnp.stack([jnp.tile(q_norm_a[j] * scale, 2), jnp.tile(k_norm_a[j], 2)])
            q, k, v = _proj_a(x, mod[i], norm1_g[i].reshape(1, d), w, gains, gsum, cos, sup, sdn)
            o = _attn_a(sinks_a[j], q, k, v)
            wo = wo_a[j][perm, :].astype(BF16)
        else:
            w = wqkv_b[j]
            nqb = N_H_B * HEAD_DIM
            w = jnp.concatenate([w[:, :nqb] * scale, w[:, nqb:]], axis=1).astype(BF16)
            o = _attn_b(_proj_b(x, mod[i], norm1_g[i].reshape(1, d), w), uo)
            wo = wo_b[j].astype(BF16)
        wg = w_gate[i].astype(BF16).reshape(d, nf, FF_CHUNK).transpose(1, 0, 2)
        wu = w_up[i].astype(BF16).reshape(d, nf, FF_CHUNK).transpose(1, 0, 2)
        wd = w_down[i].astype(BF16).reshape(nf, FF_CHUNK, d)
        x = _post(o, x, mod[i], norm2_g[i].reshape(1, d), wo, wg, wu, wd)
    return x
```

```python
import functools

import jax
import jax.numpy as jnp
from jax import lax
from jax.experimental import pallas as pl
from jax.experimental.pallas import tpu as pltpu

F32 = jnp.float32
BF16 = jnp.bfloat16

HEAD_DIM = 64
N_Q_A = 16
N_KV_A = 2
GROUP_A = N_Q_A // N_KV_A
N_H_B = 16
BLOCK = 128
WINDOW = 128
ROT_DIM = HEAD_DIM // 4
ROPE_THETA = 500000.0
EPS = 1e-6
LANES = 128
FF_CHUNK = 256
ROW_TILE = 512
VMEM_LIMIT = 56 * 1024 * 1024
SB_CUTOFF = -104.0
SB_BLOCKS = 3
SB_GROUP = 5
LOG2E = 1.4426950408889634

_NT = (((1,), (1,)), ((), ()))


def _params(sem):
    return pltpu.CompilerParams(dimension_semantics=sem, vmem_limit_bytes=VMEM_LIMIT)


def _resident(shape):
    return pl.BlockSpec(shape, lambda *_: (0,) * len(shape), pipeline_mode=pl.Buffered(1))


def _rms_mod(x, gain, scale, shift):
    ms = jnp.mean(x * x, axis=-1, keepdims=True)
    y = x * lax.rsqrt(ms + EPS)
    return (y * gain) * (1.0 + scale) + shift


def _split_dot(x, m):
    hi = x.astype(BF16)
    lo = (x - hi.astype(F32)).astype(BF16)
    return (jnp.dot(hi, m, preferred_element_type=F32)
            + jnp.dot(lo, m, preferred_element_type=F32))


def _adaln_kernel(c_ref, w_ref, b_ref, o_ref):
    c = c_ref[...]
    cond = (c * jax.nn.sigmoid(c)).astype(BF16)
    o_ref[0] = jnp.dot(cond, w_ref[0].astype(BF16), preferred_element_type=F32) + b_ref[0]


def _adaln(c, ada_w, ada_b):
    depth, d, n = ada_w.shape
    b = c.shape[0]
    tn = 1024
    return pl.pallas_call(
        _adaln_kernel,
        out_shape=jax.ShapeDtypeStruct((depth, b, n), F32),
        grid=(depth, n // tn),
        in_specs=[
            pl.BlockSpec((b, d), lambda l, j: (0, 0)),
            pl.BlockSpec((1, d, tn), lambda l, j: (l, 0, j)),
            pl.BlockSpec((1, 1, tn), lambda l, j: (l, 0, j)),
        ],
        out_specs=pl.BlockSpec((1, b, tn), lambda l, j: (l, 0, j)),
        compiler_params=_params(("arbitrary", "arbitrary")),
        name="adaln",
    )(c, ada_w, ada_b.reshape(depth, 1, n))


def _rope_kernel(pos_ref, invf_ref, cos_ref, sin_ref, nsin_ref):
    ang = pos_ref[...].astype(F32) * invf_ref[...]
    s = jnp.sin(ang)
    cos_ref[...] = jnp.cos(ang)
    sin_ref[...] = s
    nsin_ref[...] = -s


def _rope_tables(positions):
    b, s = positions.shape
    t = b * s
    half = ROT_DIM // 2
    inv_freq = jnp.power(F32(ROPE_THETA), -jnp.arange(half, dtype=F32) * 2.0 / ROT_DIM)
    tl = min(t, 8192)
    shp = jax.ShapeDtypeStruct((half, t), F32)
    cos, sin, nsin = pl.pallas_call(
        _rope_kernel,
        out_shape=(shp, shp, shp),
        grid=(t // tl,),
        in_specs=[pl.BlockSpec((1, tl), lambda i: (0, i)),
                  pl.BlockSpec((half, 1), lambda i: (0, 0))],
        out_specs=(pl.BlockSpec((half, tl), lambda i: (0, i)),) * 3,
        compiler_params=_params(("arbitrary",)),
        name="rope_tables",
    )(positions.reshape(1, t), inv_freq.reshape(half, 1))
    cos, sin, nsin = cos.T, sin.T, nsin.T
    one = jnp.ones((t, HEAD_DIM - ROT_DIM), F32)
    zero = jnp.zeros((t, HEAD_DIM - ROT_DIM), F32)
    zh = jnp.zeros((t, half), F32)
    cos_full = jnp.concatenate([cos, cos, one] * 2, axis=1)
    sin_up = jnp.concatenate([nsin, zh, zero] * 2, axis=1)
    sin_dn = jnp.concatenate([zh, sin, zero] * 2, axis=1)
    return (cos_full.reshape(b, s, LANES), sin_up.reshape(b, s, LANES),
            sin_dn.reshape(b, s, LANES))


def _proj_a_kernel(x_ref, mod_ref, g_ref, w_ref, gain_ref, gsum_ref, cos_ref, sup_ref, sdn_ref,
                   q_ref, k_ref, v_ref):
    h = _rms_mod(x_ref[0], g_ref[...], mod_ref[0, 1:2, :], mod_ref[0, 0:1, :]).astype(BF16)
    qkv = jnp.dot(h, w_ref[...], preferred_element_type=F32)
    cos, sup, sdn = cos_ref[0], sup_ref[0], sdn_ref[0]
    gsum = gsum_ref[...]
    half = ROT_DIM // 2

    def norm_rope(blk, gain):
        ssq = _split_dot(blk * blk, gsum)
        y = blk * lax.rsqrt(ssq * (1.0 / HEAD_DIM) + EPS) * gain
        return (y * cos + pltpu.roll(y, LANES - half, 1) * sup + pltpu.roll(y, half, 1) * sdn)

    nq = q_ref.shape[-1] // LANES
    for cb in range(nq):
        blk = qkv[:, cb * LANES:(cb + 1) * LANES]
        q_ref[0, :, cb * LANES:(cb + 1) * LANES] = norm_rope(blk, gain_ref[0:1, :]).astype(BF16)
    k_ref[0] = norm_rope(qkv[:, nq * LANES:(nq + 1) * LANES], gain_ref[1:2, :]).astype(BF16)
    v_ref[0] = qkv[:, (nq + 1) * LANES:(nq + 2) * LANES].astype(BF16)


def _proj_a(x, mod, gain1, w, gains, gsum, cos, sup, sdn):
    b, s, d = x.shape
    tm = ROW_TILE
    nq = N_Q_A * HEAD_DIM
    row = lambda n: pl.BlockSpec((1, tm, n), lambda bi, i: (bi, i, 0))
    out = lambda n: jax.ShapeDtypeStruct((b, s, n), BF16)
    return pl.pallas_call(
        _proj_a_kernel,
        out_shape=(out(nq), out(LANES), out(LANES)),
        grid=(b, s // tm),
        in_specs=[
            row(d),
            pl.BlockSpec((1, 6, d), lambda bi, i: (bi, 0, 0)),
            _resident((1, d)),
            _resident(w.shape),
            _resident(gains.shape),
            _resident(gsum.shape),
            row(LANES), row(LANES), row(LANES),
        ],
        out_specs=(row(nq), row(LANES), row(LANES)),
        compiler_params=_params(("parallel", "parallel")),
        name="proj_a",
    )(x, mod, gain1, w, gains, gsum, cos, sup, sdn)


def _attn_a_kernel(sink_ref, q_ref, kc_ref, kp_ref, vc_ref, vp_ref, o_ref):
    i = pl.program_id(1)
    kcat = jnp.concatenate([kp_ref[0], kc_ref[0]], axis=0)
    vcat = jnp.concatenate([vp_ref[0], vc_ref[0]], axis=0)
    r = lax.broadcasted_iota(jnp.int32, (2 * BLOCK, 2 * BLOCK), 0)
    c = lax.broadcasted_iota(jnp.int32, (2 * BLOCK, 2 * BLOCK), 1)
    rel = (r & (BLOCK - 1)) + BLOCK - c
    valid = (rel >= 0) & (rel < WINDOW) & (c + (i - 1) * BLOCK >= 0)
    lane2 = lax.broadcasted_iota(jnp.int32, (2 * BLOCK, LANES), 1)
    row2 = lax.broadcasted_iota(jnp.int32, (2 * BLOCK, LANES), 0)
    keep = (lane2 < HEAD_DIM) == (row2 < BLOCK)
    rowc = lax.broadcasted_iota(jnp.int32, (2 * BLOCK, 1), 0)
    lane1 = lax.broadcasted_iota(jnp.int32, (BLOCK, LANES), 1)
    for p in range(GROUP_A):
        qp = q_ref[0, :, p * LANES:(p + 1) * LANES]
        qq = jnp.concatenate([qp, qp], axis=0)
        lhs = jnp.where(keep, qq, jnp.zeros_like(qq))
        s = lax.dot_general(lhs, kcat, _NT, preferred_element_type=F32)
        s = jnp.where(valid, s, -1e30)
        sink = jnp.where(rowc < BLOCK, sink_ref[p], sink_ref[p + GROUP_A])
        m = jnp.maximum(jnp.max(s, axis=-1, keepdims=True), sink)
        e = jnp.exp(s - m)
        denom = jnp.sum(e, axis=-1, keepdims=True) + jnp.exp(sink - m)
        pv = jnp.dot(e.astype(BF16), vcat, preferred_element_type=F32) * (1.0 / denom)
        o_ref[0, :, p * LANES:(p + 1) * LANES] = jnp.where(
            lane1 < HEAD_DIM, pv[:BLOCK], pv[BLOCK:]).astype(BF16)


def _attn_a(sinks, q, k, v):
    b, s, nq = q.shape
    cur = lambda bi, i: (bi, i, 0)
    prev = lambda bi, i: (bi, jnp.maximum(i - 1, 0), 0)
    return pl.pallas_call(
        _attn_a_kernel,
        out_shape=jax.ShapeDtypeStruct((b, s, nq), BF16),
        grid=(b, s // BLOCK),
        in_specs=[
            pl.BlockSpec(memory_space=pltpu.SMEM),
            pl.BlockSpec((1, BLOCK, nq), cur),
            pl.BlockSpec((1, BLOCK, LANES), cur),
            pl.BlockSpec((1, BLOCK, LANES), prev),
            pl.BlockSpec((1, BLOCK, LANES), cur),
            pl.BlockSpec((1, BLOCK, LANES), prev),
        ],
        out_specs=pl.BlockSpec((1, BLOCK, nq), cur),
        compiler_params=_params(("parallel", "parallel")),
        name="attn_a",
    )(sinks, q, k, k, v, v)


def _proj_b_kernel(x_ref, mod_ref, g_ref, w_ref, o_ref):
    h = _rms_mod(x_ref[0], g_ref[...], mod_ref[0, 1:2, :], mod_ref[0, 0:1, :]).astype(BF16)
    d = w_ref.shape[0]
    for n in range(w_ref.shape[1] // d):
        o_ref[0, :, n * d:(n + 1) * d] = jnp.dot(
            h, w_ref[:, n * d:(n + 1) * d], preferred_element_type=F32).astype(BF16)


def _proj_b(x, mod, gain1, w):
    b, s, d = x.shape
    tm = ROW_TILE
    n = w.shape[1]
    return pl.pallas_call(
        _proj_b_kernel,
        out_shape=jax.ShapeDtypeStruct((b, s, n), BF16),
        grid=(b, s // tm),
        in_specs=[
            pl.BlockSpec((1, tm, d), lambda bi, i: (bi, i, 0)),
            pl.BlockSpec((1, 6, d), lambda bi, i: (bi, 0, 0)),
            _resident((1, d)),
            _resident(w.shape),
        ],
        out_specs=pl.BlockSpec((1, tm, n), lambda bi, i: (bi, i, 0)),
        compiler_params=_params(("parallel", "parallel")),
        name="proj_b",
    )(x, mod, gain1, w)


def _aligned(v):
    return v if isinstance(v, int) else pl.multiple_of(v, BLOCK)


def _attn_b_kernel(q_ref, k_ref, v_ref, uo_ref, o_ref):
    uo = uo_ref[...]
    lane = lax.broadcasted_iota(jnp.int32, (BLOCK, LANES), 1)
    strict = lane < lax.broadcasted_iota(jnp.int32, (BLOCK, LANES), 0)
    head0 = lane < HEAD_DIM
    nq = q_ref.shape[1] // BLOCK

    def split_heads(q):
        return jnp.where(head0, q, jnp.zeros_like(q)), jnp.where(head0, jnp.zeros_like(q), q)

    def sweep(chains):
        nb = len(chains[0][2])
        vals = [v_ref[0, pl.ds(start, nb * BLOCK), :] for _, start, _, _ in chains]
        zs = [lax.dot_general(qm, k_ref[0, pl.ds(start, nb * BLOCK), :], _NT,
                              preferred_element_type=F32) * LOG2E for qm, start, _, _ in chains]
        log_beta = [[None] * nb for _ in chains]
        sums = [[None] * nb for _ in chains]
        for b in reversed(range(nb)):
            for c, (_, _, masks, _) in enumerate(chains):
                zb = zs[c][:, b * BLOCK:(b + 1) * BLOCK]
                lb = jnp.minimum(zb, 0.0) - jnp.log2(1.0 + jnp.exp2(-jnp.abs(zb)))
                log_rest = lb - zb
                if masks[b] is not None:
                    log_rest = jnp.where(masks[b], log_rest, 0.0)
                hi = log_rest.astype(BF16)
                lo = (log_rest - hi.astype(F32)).astype(BF16)
                sums[c][b] = jnp.dot(jnp.concatenate([hi, lo], axis=1), uo,
                                     preferred_element_type=F32)
                log_beta[c][b] = lb
        out = []
        for c, (_, _, masks, carry) in enumerate(chains):
            later = carry
            weights = [None] * nb
            for b in reversed(range(nb)):
                t = log_beta[c][b] + sums[c][b][:, :BLOCK]
                if later is not None:
                    t = t + later
                a = jnp.exp2(t)
                if masks[b] is not None:
                    a = jnp.where(masks[b], a, 0.0)
                weights[b] = a.astype(BF16)
                total = sums[c][b][:, BLOCK:]
                later = total if later is None else later + total
            out.append((later, jnp.dot(jnp.concatenate(weights, axis=1), vals[c],
                                       preferred_element_type=F32)))
        return out

    def first_windows(blocks, nb):
        chains, starts = [], []
        masks = [None] * (nb - 1) + [strict]
        for i in blocks:
            qs = _aligned(i * BLOCK)
            start = _aligned(qs - (nb - 1) * BLOCK)
            q0, q1 = split_heads(q_ref[0, pl.ds(qs, BLOCK), :])
            chains += [(q0, start, masks, None), (q1, start, masks, None)]
            starts.append(start)
        res = sweep(chains)
        return [(start, res[2 * m], res[2 * m + 1]) for m, start in enumerate(starts)]

    def live(c0, c1):
        return (jnp.maximum(jnp.max(c0), jnp.max(c1)) > SB_CUTOFF * LOG2E).astype(jnp.int32)

    def store(i, a0, a1):
        o_ref[0, pl.ds(_aligned(i * BLOCK), BLOCK), :] = jnp.where(head0, a0, a1).astype(BF16)

    def sweep_rest(i, start, flag, res0, res1):
        qs = _aligned(i * BLOCK)

        def cond(st):
            return (st[0] > 0) & (st[1] > 0)

        def body(st):
            end, _, c0, a0, c1, a1 = st
            begin = pl.multiple_of(jnp.maximum(end - SB_BLOCKS * BLOCK, 0), BLOCK)
            masks = [lane < end - begin - b * BLOCK for b in range(SB_BLOCKS)]
            q0, q1 = split_heads(q_ref[0, pl.ds(qs, BLOCK), :])
            (c0, p0), (c1, p1) = sweep([(q0, begin, masks, c0), (q1, begin, masks, c1)])
            return begin, live(c0, c1), c0, a0 + p0, c1, a1 + p1

        (c0, a0), (c1, a1) = res0, res1
        st = lax.while_loop(cond, body, (start, flag, c0, a0, c1, a1))
        store(i, st[3], st[5])

    for i in range(SB_BLOCKS - 1):
        (_, (_, a0), (_, a1)), = first_windows([i], i + 1)
        store(i, a0, a1)

    def group(g, _):
        first = (SB_BLOCKS - 1) + g * SB_GROUP
        wins = first_windows([first + m for m in range(SB_GROUP)], SB_BLOCKS)
        flags = []
        for m, (_, (c0, a0), (c1, a1)) in enumerate(wins):
            store(first + m, a0, a1)
            flags.append(live(c0, c1))

        @pl.when(functools.reduce(jnp.maximum, flags) > 0)
        def _():
            for m, (start, r0, r1) in enumerate(wins):
                sweep_rest(first + m, start, flags[m], r0, r1)

        return 0

    lax.fori_loop(0, (nq - (SB_BLOCKS - 1)) // SB_GROUP, group, 0)


def _attn_b(qkv):
    b, s, n = qkv.shape
    npair = n // 3 // LANES
    assert (s // BLOCK - (SB_BLOCKS - 1)) % SB_GROUP == 0
    key = jnp.arange(BLOCK)
    later = (key[:, None] > key[None, :]).astype(BF16)
    uo = jnp.concatenate([later, jnp.ones((BLOCK, BLOCK), BF16)], axis=1)
    uo = jnp.concatenate([uo, uo], axis=0)
    blk = lambda off: pl.BlockSpec((1, s, LANES), lambda bi, hp: (bi, 0, off + hp))
    return pl.pallas_call(
        _attn_b_kernel,
        out_shape=jax.ShapeDtypeStruct((b, s, n // 3), BF16),
        grid=(b, npair),
        in_specs=[blk(0), blk(npair), blk(2 * npair), _resident(uo.shape)],
        out_specs=blk(0),
        compiler_params=_params(("parallel", "parallel")),
        name="attn_b",
    )(qkv, qkv, qkv, uo)


def _post_kernel(o_ref, x_ref, mod_ref, g_ref, wo_ref, wg_ref, wu_ref, wd_ref, out_ref, acc_ref):
    y = jnp.dot(o_ref[0], wo_ref[...], preferred_element_type=F32)
    x1 = x_ref[0] + mod_ref[0, 2:3, :] * y
    h = _rms_mod(x1, g_ref[...], mod_ref[0, 4:5, :], mod_ref[0, 3:4, :]).astype(BF16)
    for f in range(wg_ref.shape[0]):
        g = jnp.dot(h, wg_ref[f], preferred_element_type=F32)
        u = jnp.dot(h, wu_ref[f], preferred_element_type=F32)
        act = ((g * jax.nn.sigmoid(g)) * u).astype(BF16)
        down = jnp.dot(act, wd_ref[f], preferred_element_type=F32)
        if f == 0:
            acc_ref[...] = down
        else:
            acc_ref[...] += down
    out_ref[0] = x1 + mod_ref[0, 5:6, :] * acc_ref[...]


def _post(o, x, mod, gain2, wo, wg, wu, wd):
    b, s, d = x.shape
    tm = ROW_TILE
    row = lambda n: pl.BlockSpec((1, tm, n), lambda bi, i: (bi, i, 0))
    return pl.pallas_call(
        _post_kernel,
        out_shape=jax.ShapeDtypeStruct((b, s, d), F32),
        grid=(b, s // tm),
        in_specs=[
            row(o.shape[-1]), row(d),
            pl.BlockSpec((1, 6, d), lambda bi, i: (bi, 0, 0)),
            _resident((1, d)),
            _resident(wo.shape), _resident(wg.shape), _resident(wu.shape), _resident(wd.shape),
        ],
        out_specs=row(d),
        scratch_shapes=[pltpu.VMEM((tm, d), F32)],
        compiler_params=_params(("parallel", "parallel")),
        name="post",
    )(o, x, mod, gain2, wo, wg, wu, wd)


def _pair_perm():
    cols = []
    for p in range(GROUP_A):
        for h in (p, p + GROUP_A):
            cols.extend(range(h * HEAD_DIM, (h + 1) * HEAD_DIM))
    return jnp.asarray(cols, jnp.int32)


def kernel(x, c, positions, ada_w, ada_b, norm1_g, norm2_g, wqkv_a, q_norm_a, k_norm_a, sinks_a,
           wo_a, wqkv_b, wo_b, w_gate, w_up, w_down):
    b, s, d = x.shape
    depth = ada_w.shape[0]
    d_ff = w_gate.shape[-1]
    nf = d_ff // FF_CHUNK
    scale = HEAD_DIM ** -0.5

    mod = _adaln(c, ada_w, ada_b).reshape(depth, b, 6, d)
    cos, sup, sdn = _rope_tables(positions)

    perm = _pair_perm()
    nqa = N_Q_A * HEAD_DIM
    lane = jnp.arange(LANES)
    gsum = (lane[:, None] // HEAD_DIM == lane[None, :] // HEAD_DIM).astype(BF16)

    for i in range(depth):
        j = i // 2
        if i % 2 == 0:
            w = wqkv_a[j]
            w = jnp.concatenate([w[:, :nqa][:, perm], w[:, nqa:]], axis=1).astype(BF16)
            gains = jnp.stack([jnp.tile(q_norm_a[j] * scale, 2), jnp.tile(k_norm_a[j], 2)])
            q, k, v = _proj_a(x, mod[i], norm1_g[i].reshape(1, d), w, gains, gsum, cos, sup, sdn)
            o = _attn_a(sinks_a[j], q, k, v)
            wo = wo_a[j][perm, :].astype(BF16)
        else:
            w = wqkv_b[j]
            nqb = N_H_B * HEAD_DIM
            w = jnp.concatenate([w[:, :nqb] * scale, w[:, nqb:]], axis=1).astype(BF16)
            o = _attn_b(_proj_b(x, mod[i], norm1_g[i].reshape(1, d), w))
            wo = wo_b[j].astype(BF16)
        wg = w_gate[i].astype(BF16).reshape(d, nf, FF_CHUNK).transpose(1, 0, 2)
        wu = w_up[i].astype(BF16).reshape(d, nf, FF_CHUNK).transpose(1, 0, 2)
        wd = w_down[i].astype(BF16).reshape(nf, FF_CHUNK, d)
        x = _post(o, x, mod[i], norm2_g[i].reshape(1, d), wo, wg, wu, wd)
    return x
```

```python
import functools

import jax
import jax.numpy as jnp
from jax import lax
from jax.experimental import pallas as pl
from jax.experimental.pallas import tpu as pltpu

F32 = jnp.float32
BF16 = jnp.bfloat16

HEAD_DIM = 64
N_Q_A = 16
N_KV_A = 2
GROUP_A = N_Q_A // N_KV_A
N_H_B = 16
BLOCK = 128
WINDOW = 128
assert WINDOW == BLOCK
ROT_DIM = HEAD_DIM // 4
ROPE_THETA = 500000.0
EPS = 1e-6
LANES = 128
FF_CHUNK = 256
ROW_TILE = 512
VMEM_LIMIT = 56 * 1024 * 1024
SB_CUTOFF = -104.0
SB_ROWS = 64
SB_KEYS = 256
SB_GROUP = 10
LOG2E = 1.4426950408889634

_NT = (((1,), (1,)), ((), ()))


def _params(sem):
    return pltpu.CompilerParams(dimension_semantics=sem, vmem_limit_bytes=VMEM_LIMIT)


def _resident(shape):
    return pl.BlockSpec(shape, lambda *_: (0,) * len(shape), pipeline_mode=pl.Buffered(1))


def _rms_mod(x, gain, scale, shift):
    ms = jnp.mean(x * x, axis=-1, keepdims=True)
    y = x * lax.rsqrt(ms + EPS)
    return (y * gain) * (1.0 + scale) + shift


def _split_dot(x, m):
    hi = x.astype(BF16)
    lo = (x - hi.astype(F32)).astype(BF16)
    return (jnp.dot(hi, m, preferred_element_type=F32)
            + jnp.dot(lo, m, preferred_element_type=F32))


def _adaln_kernel(c_ref, w_ref, b_ref, o_ref):
    c = c_ref[...]
    cond = (c * jax.nn.sigmoid(c)).astype(BF16)
    o_ref[0] = jnp.dot(cond, w_ref[0].astype(BF16), preferred_element_type=F32) + b_ref[0]


def _adaln(c, ada_w, ada_b):
    depth, d, n = ada_w.shape
    b = c.shape[0]
    tn = 1024
    return pl.pallas_call(
        _adaln_kernel,
        out_shape=jax.ShapeDtypeStruct((depth, b, n), F32),
        grid=(depth, n // tn),
        in_specs=[
            pl.BlockSpec((b, d), lambda l, j: (0, 0)),
            pl.BlockSpec((1, d, tn), lambda l, j: (l, 0, j)),
            pl.BlockSpec((1, 1, tn), lambda l, j: (l, 0, j)),
        ],
        out_specs=pl.BlockSpec((1, b, tn), lambda l, j: (l, 0, j)),
        compiler_params=_params(("arbitrary", "arbitrary")),
        name="adaln",
    )(c, ada_w, ada_b.reshape(depth, 1, n))


def _rope_consts():
    half = ROT_DIM // 2
    inv_freq = jnp.power(F32(ROPE_THETA), -jnp.arange(half, dtype=F32) * 2.0 / ROT_DIM)
    f = jnp.arange(half)[:, None]
    d = jnp.arange(LANES)[None, :] % HEAD_DIM
    place = jnp.stack([(d == f) | (d == f + half), d == f, d == f + half]).astype(F32)
    place = place * jnp.asarray([1.0, -1.0, 1.0], F32)[:, None, None]
    base = (d >= ROT_DIM).astype(F32)
    return inv_freq.reshape(half, 1), place, base


def _proj_a_kernel(x_ref, mod_ref, g_ref, w_ref, gain_ref, gsum_ref, pos_ref, invf_ref, place_ref,
                   base_ref, q_ref, k_ref, v_ref):
    h = _rms_mod(x_ref[0], g_ref[...], mod_ref[0, 1:2, :], mod_ref[0, 0:1, :]).astype(BF16)
    qkv = jnp.dot(h, w_ref[...], preferred_element_type=F32)
    ang = pos_ref[0].astype(F32) * invf_ref[...]
    cos_t, sin_t = jnp.cos(ang), jnp.sin(ang)
    expand = lambda t, m: lax.dot_general(t, place_ref[m], (((0,), (0,)), ((), ())),
                                          precision=lax.Precision.HIGHEST,
                                          preferred_element_type=F32)
    cos, sup, sdn = expand(cos_t, 0) + base_ref[...], expand(sin_t, 1), expand(sin_t, 2)
    gsum = gsum_ref[...]
    half = ROT_DIM // 2

    def norm_rope(blk, gain):
        ssq = _split_dot(blk * blk, gsum)
        y = blk * lax.rsqrt(ssq * (1.0 / HEAD_DIM) + EPS) * gain
        return (y * cos + pltpu.roll(y, LANES - half, 1) * sup + pltpu.roll(y, half, 1) * sdn)

    nq = q_ref.shape[-1] // LANES
    for cb in range(nq):
        blk = qkv[:, cb * LANES:(cb + 1) * LANES]
        q_ref[0, :, cb * LANES:(cb + 1) * LANES] = norm_rope(blk, gain_ref[0:1, :]).astype(BF16)
    k_ref[0] = norm_rope(qkv[:, nq * LANES:(nq + 1) * LANES], gain_ref[1:2, :]).astype(BF16)
    v_ref[0] = qkv[:, (nq + 1) * LANES:(nq + 2) * LANES].astype(BF16)


def _proj_a(x, mod, gain1, w, gains, gsum, positions):
    b, s, d = x.shape
    tm = ROW_TILE
    nq = N_Q_A * HEAD_DIM
    inv_freq, place, base = _rope_consts()
    row = lambda n: pl.BlockSpec((1, tm, n), lambda bi, i: (bi, i, 0))
    out = lambda n: jax.ShapeDtypeStruct((b, s, n), BF16)
    return pl.pallas_call(
        _proj_a_kernel,
        out_shape=(out(nq), out(LANES), out(LANES)),
        grid=(b, s // tm),
        in_specs=[
            row(d),
            pl.BlockSpec((1, 6, d), lambda bi, i: (bi, 0, 0)),
            _resident((1, d)),
            _resident(w.shape),
            _resident(gains.shape),
            _resident(gsum.shape),
            pl.BlockSpec((1, 1, tm), lambda bi, i: (bi, 0, i)),
            _resident(inv_freq.shape), _resident(place.shape), _resident(base.shape),
        ],
        out_specs=(row(nq), row(LANES), row(LANES)),
        compiler_params=_params(("parallel", "parallel")),
        name="proj_a",
    )(x, mod, gain1, w, gains, gsum, positions.reshape(b, 1, s), inv_freq, place, base)


def _attn_a_kernel(sink_ref, q_ref, kc_ref, kp_ref, vc_ref, vp_ref, o_ref):
    i = pl.program_id(1)
    kcat = jnp.concatenate([kp_ref[0], kc_ref[0]], axis=0)
    vcat = jnp.concatenate([vp_ref[0], vc_ref[0]], axis=0)
    lane2 = lax.broadcasted_iota(jnp.int32, (2 * BLOCK, LANES), 1)
    row2 = lax.broadcasted_iota(jnp.int32, (2 * BLOCK, LANES), 0)
    keep = (lane2 < HEAD_DIM) == (row2 < BLOCK)
    own = lane2 <= (row2 & (BLOCK - 1))
    no_prev = jnp.where(i > 0, 0.0, -1e30).astype(F32)
    rowc = lax.broadcasted_iota(jnp.int32, (2 * BLOCK, 1), 0)
    lane1 = lax.broadcasted_iota(jnp.int32, (BLOCK, LANES), 1)
    scores = []
    for p in range(GROUP_A):
        qp = q_ref[0, :, p * LANES:(p + 1) * LANES]
        qq = jnp.concatenate([qp, qp], axis=0)
        lhs = jnp.where(keep, qq, jnp.zeros_like(qq))
        scores.append(lax.dot_general(lhs, kcat, _NT, preferred_element_type=F32))
    outs = []
    for p in range(GROUP_A):
        s = jnp.where(own, scores[p][:, BLOCK:], scores[p][:, :BLOCK] + no_prev)
        sink = jnp.where(rowc < BLOCK, sink_ref[p], sink_ref[p + GROUP_A])
        m = jnp.maximum(jnp.max(s, axis=-1, keepdims=True), sink)
        e = jnp.exp(s - m)
        denom = jnp.sum(e, axis=-1, keepdims=True) + jnp.exp(sink - m)
        e = e.astype(BF16)
        zero = jnp.zeros_like(e)
        weights = jnp.concatenate([jnp.where(own, zero, e), jnp.where(own, e, zero)], axis=1)
        outs.append((jnp.dot(weights, vcat, preferred_element_type=F32), denom))
    for p, (pv, denom) in enumerate(outs):
        pv = pv * (1.0 / denom)
        o_ref[0, :, p * LANES:(p + 1) * LANES] = jnp.where(
            lane1 < HEAD_DIM, pv[:BLOCK], pv[BLOCK:]).astype(BF16)


def _attn_a(sinks, q, k, v):
    b, s, nq = q.shape
    cur = lambda bi, i: (bi, i, 0)
    prev = lambda bi, i: (bi, jnp.maximum(i - 1, 0), 0)
    return pl.pallas_call(
        _attn_a_kernel,
        out_shape=jax.ShapeDtypeStruct((b, s, nq), BF16),
        grid=(b, s // BLOCK),
        in_specs=[
            pl.BlockSpec(memory_space=pltpu.SMEM),
            pl.BlockSpec((1, BLOCK, nq), cur),
            pl.BlockSpec((1, BLOCK, LANES), cur),
            pl.BlockSpec((1, BLOCK, LANES), prev),
            pl.BlockSpec((1, BLOCK, LANES), cur),
            pl.BlockSpec((1, BLOCK, LANES), prev),
        ],
        out_specs=pl.BlockSpec((1, BLOCK, nq), cur),
        compiler_params=_params(("parallel", "parallel")),
        name="attn_a",
    )(sinks, q, k, k, v, v)


def _proj_b_kernel(x_ref, mod_ref, g_ref, w_ref, o_ref):
    h = _rms_mod(x_ref[0], g_ref[...], mod_ref[0, 1:2, :], mod_ref[0, 0:1, :]).astype(BF16)
    d = w_ref.shape[0]
    for n in range(w_ref.shape[1] // d):
        o_ref[0, :, n * d:(n + 1) * d] = jnp.dot(
            h, w_ref[:, n * d:(n + 1) * d], preferred_element_type=F32).astype(BF16)


def _proj_b(x, mod, gain1, w):
    b, s, d = x.shape
    tm = ROW_TILE
    n = w.shape[1]
    return pl.pallas_call(
        _proj_b_kernel,
        out_shape=jax.ShapeDtypeStruct((b, s, n), BF16),
        grid=(b, s // tm),
        in_specs=[
            pl.BlockSpec((1, tm, d), lambda bi, i: (bi, i, 0)),
            pl.BlockSpec((1, 6, d), lambda bi, i: (bi, 0, 0)),
            _resident((1, d)),
            _resident(w.shape),
        ],
        out_specs=pl.BlockSpec((1, tm, n), lambda bi, i: (bi, i, 0)),
        compiler_params=_params(("parallel", "parallel")),
        name="proj_b",
    )(x, mod, gain1, w)


def _attn_b_kernel(q_ref, k_ref, v_ref, uo_ref, o_ref):
    uo = uo_ref[...]
    nrow = 2 * SB_ROWS
    lane = lax.broadcasted_iota(jnp.int32, (nrow, LANES), 1)
    row = lax.broadcasted_iota(jnp.int32, (nrow, LANES), 0)
    keep = (lane < HEAD_DIM) == (row < SB_ROWS)
    col_minus_row = lane - (row & (SB_ROWS - 1))
    out_head0 = lax.broadcasted_iota(jnp.int32, (SB_ROWS, LANES), 1) < HEAD_DIM
    cutoff = SB_CUTOFF * LOG2E
    sign = jnp.int32(-2 ** 31)

    def stacked_queries(t0):
        q = q_ref[0, pl.ds(t0, SB_ROWS), :]
        qq = jnp.concatenate([q, q], axis=0)
        return jnp.where(keep, qq, jnp.zeros_like(qq))

    def scores(lhs, start, nkeys):
        return lax.dot_general(lhs, k_ref[0, pl.ds(start, nkeys), :], _NT,
                               preferred_element_type=F32) * LOG2E

    def log_terms(z, mask):
        neg_abs = lax.bitcast_convert_type(lax.bitcast_convert_type(z, jnp.int32) | sign, F32)
        log_beta = jnp.minimum(z, 0.0) - jnp.log2(1.0 + jnp.exp2(neg_abs))
        log_rest = log_beta - z
        if mask is not None:
            log_rest = jnp.where(mask, log_rest, 0.0)
        hi = log_rest.astype(BF16)
        lo = (log_rest - hi.astype(F32)).astype(BF16)
        return log_beta, jnp.dot(jnp.concatenate([hi, lo], axis=1), uo,
                                 preferred_element_type=F32)

    def weights(log_beta, sums, later, mask):
        t = log_beta + sums[:, :BLOCK]
        if later is not None:
            t = t + later
        a = jnp.exp2(t)
        if mask is not None:
            a = jnp.where(mask, a, 0.0)
        return a.astype(BF16)

    def store(t0, pv):
        o_ref[0, pl.ds(t0, SB_ROWS), :] = jnp.where(
            out_head0, pv[:SB_ROWS], pv[SB_ROWS:]).astype(BF16)

    def first_windows(units):
        tiles, zs, vals = [], [], []
        for t0, start, limit in units:
            mine = [(b, None if b * BLOCK + BLOCK - 1 < limit else col_minus_row < limit - b * BLOCK)
                    for b in range(SB_KEYS // BLOCK) if b * BLOCK - (SB_ROWS - 1) < limit]
            tiles.append(mine)
            zs.append(scores(stacked_queries(t0), start, len(mine) * BLOCK))
            vals.append(v_ref[0, pl.ds(start, len(mine) * BLOCK), :])
        terms = [dict() for _ in units]
        for b in reversed(range(SB_KEYS // BLOCK)):
            for u, mine in enumerate(tiles):
                for tb, mask in mine:
                    if tb == b:
                        terms[u][b] = log_terms(zs[u][:, b * BLOCK:(b + 1) * BLOCK], mask)
        totals = []
        for u, (t0, _, _) in enumerate(units):
            later, ws = None, []
            for b, mask in reversed(tiles[u]):
                log_beta, sums = terms[u][b]
                ws.insert(0, weights(log_beta, sums, later, mask))
                later = sums[:, BLOCK:] if later is None else later + sums[:, BLOCK:]
            store(t0, jnp.dot(jnp.concatenate(ws, axis=1), vals[u], preferred_element_type=F32))
            totals.append(later)
        return totals

    def redo(t0):
        lhs = stacked_queries(t0)
        zeros = jnp.zeros((nrow, BLOCK), F32)

        def cond(st):
            return (st[0] >= 0) & (st[1] > 0)

        def body(st):
            j, _, carry, acc = st
            start = pl.multiple_of(j * BLOCK, BLOCK)
            mask = col_minus_row < t0 - start
            log_beta, sums = log_terms(scores(lhs, start, BLOCK), mask)
            w = weights(log_beta, sums, carry, mask)
            acc = acc + jnp.dot(w, v_ref[0, pl.ds(start, BLOCK), :], preferred_element_type=F32)
            carry = carry + sums[:, BLOCK:]
            return j - 1, (jnp.max(carry) > cutoff).astype(jnp.int32), carry, acc

        st = lax.while_loop(cond, body, (t0 // BLOCK, jnp.int32(1), zeros, zeros))
        store(t0, st[3])

    nunit = q_ref.shape[1] // SB_ROWS
    nclamp = SB_KEYS // SB_ROWS - 1
    npeel = nclamp + (nunit - nclamp) % SB_GROUP
    first_windows([(u * SB_ROWS, max(u * SB_ROWS + SB_ROWS - SB_KEYS, 0),
                    u * SB_ROWS - max(u * SB_ROWS + SB_ROWS - SB_KEYS, 0)) for u in range(npeel)])

    def group(g, _):
        base = (npeel + g * SB_GROUP) * SB_ROWS
        t0s = [pl.multiple_of(base + m * SB_ROWS, SB_ROWS) for m in range(SB_GROUP)]
        totals = first_windows([(t0, pl.multiple_of(t0 + SB_ROWS - SB_KEYS, SB_ROWS),
                                 SB_KEYS - SB_ROWS) for t0 in t0s])
        flags = [(jnp.max(total) > cutoff).astype(jnp.int32) for total in totals]

        @pl.when(functools.reduce(jnp.maximum, flags) > 0)
        def _():
            for t0, flag in zip(t0s, flags):
                pl.when(flag > 0)(functools.partial(redo, t0))

        return 0

    lax.fori_loop(0, (nunit - npeel) // SB_GROUP, group, 0)


def _attn_b(qkv):
    b, s, n = qkv.shape
    npair = n // 3 // LANES
    assert s % SB_ROWS == 0 and s >= SB_KEYS
    key = jnp.arange(BLOCK)
    later = (key[:, None] > key[None, :]).astype(BF16)
    uo = jnp.concatenate([later, jnp.ones((BLOCK, BLOCK), BF16)], axis=1)
    uo = jnp.concatenate([uo, uo], axis=0)
    blk = lambda off: pl.BlockSpec((1, s, LANES), lambda bi, hp: (bi, 0, off + hp))
    return pl.pallas_call(
        _attn_b_kernel,
        out_shape=jax.ShapeDtypeStruct((b, s, n // 3), BF16),
        grid=(b, npair),
        in_specs=[blk(0), blk(npair), blk(2 * npair), _resident(uo.shape)],
        out_specs=blk(0),
        compiler_params=_params(("parallel", "parallel")),
        name="attn_b",
    )(qkv, qkv, qkv, uo)


def _post_kernel(o_ref, x_ref, mod_ref, g_ref, wo_ref, wg_ref, wu_ref, wd_ref, out_ref, acc_ref):
    y = jnp.dot(o_ref[0], wo_ref[...], preferred_element_type=F32)
    x1 = x_ref[0] + mod_ref[0, 2:3, :] * y
    h = _rms_mod(x1, g_ref[...], mod_ref[0, 4:5, :], mod_ref[0, 3:4, :]).astype(BF16)
    for f in range(wg_ref.shape[0]):
        g = jnp.dot(h, wg_ref[f], preferred_element_type=F32)
        u = jnp.dot(h, wu_ref[f], preferred_element_type=F32)
        act = ((g * jax.nn.sigmoid(g)) * u).astype(BF16)
        down = jnp.dot(act, wd_ref[f], preferred_element_type=F32)
        if f == 0:
            acc_ref[...] = down
        else:
            acc_ref[...] += down
    out_ref[0] = x1 + mod_ref[0, 5:6, :] * acc_ref[...]


def _post(o, x, mod, gain2, wo, wg, wu, wd):
    b, s, d = x.shape
    tm = ROW_TILE
    row = lambda n: pl.BlockSpec((1, tm, n), lambda bi, i: (bi, i, 0))
    return pl.pallas_call(
        _post_kernel,
        out_shape=jax.ShapeDtypeStruct((b, s, d), F32),
        grid=(b, s // tm),
        in_specs=[
            row(o.shape[-1]), row(d),
            pl.BlockSpec((1, 6, d), lambda bi, i: (bi, 0, 0)),
            _resident((1, d)),
            _resident(wo.shape), _resident(wg.shape), _resident(wu.shape), _resident(wd.shape),
        ],
        out_specs=row(d),
        scratch_shapes=[pltpu.VMEM((tm, d), F32)],
        compiler_params=_params(("parallel", "parallel")),
        name="post",
    )(o, x, mod, gain2, wo, wg, wu, wd)


def _pair_perm():
    cols = []
    for p in range(GROUP_A):
        for h in (p, p + GROUP_A):
            cols.extend(range(h * HEAD_DIM, (h + 1) * HEAD_DIM))
    return jnp.asarray(cols, jnp.int32)


def kernel(x, c, positions, ada_w, ada_b, norm1_g, norm2_g, wqkv_a, q_norm_a, k_norm_a, sinks_a,
           wo_a, wqkv_b, wo_b, w_gate, w_up, w_down):
    b, s, d = x.shape
    depth = ada_w.shape[0]
    d_ff = w_gate.shape[-1]
    nf = d_ff // FF_CHUNK
    scale = HEAD_DIM ** -0.5

    mod = _adaln(c, ada_w, ada_b).reshape(depth, b, 6, d)

    perm = _pair_perm()
    nqa = N_Q_A * HEAD_DIM
    lane = jnp.arange(LANES)
    gsum = (lane[:, None] // HEAD_DIM == lane[None, :] // HEAD_DIM).astype(BF16)

    for i in range(depth):
        j = i // 2
        if i % 2 == 0:
            w = wqkv_a[j]
            w = jnp.concatenate([w[:, :nqa][:, perm], w[:, nqa:]], axis=1).astype(BF16)
            gains = jnp.stack([jnp.tile(q_norm_a[j] * scale, 2), jnp.tile(k_norm_a[j], 2)])
            q, k, v = _proj_a(x, mod[i], norm1_g[i].reshape(1, d), w, gains, gsum, positions)
            o = _attn_a(sinks_a[j], q, k, v)
            wo = wo_a[j][perm, :].astype(BF16)
        else:
            w = wqkv_b[j]
            nqb = N_H_B * HEAD_DIM
            w = jnp.concatenate([w[:, :nqb] * scale, w[:, nqb:]], axis=1).astype(BF16)
            o = _attn_b(_proj_b(x, mod[i], norm1_g[i].reshape(1, d), w))
            wo = wo_b[j].astype(BF16)
        wg = w_gate[i].astype(BF16).reshape(d, nf, FF_CHUNK).transpose(1, 0, 2)
        wu = w_up[i].astype(BF16).reshape(d, nf, FF_CHUNK).transpose(1, 0, 2)
        wd = w_down[i].astype(BF16).reshape(nf, FF_CHUNK, d)
        x = _post(o, x, mod[i], norm2_g[i].reshape(1, d), wo, wg, wu, wd)
    return x
```

```python
import functools

import jax
import jax.numpy as jnp
from jax import lax
from jax.experimental import pallas as pl
from jax.experimental.pallas import tpu as pltpu

F32 = jnp.float32
BF16 = jnp.bfloat16

HEAD_DIM = 64
N_Q_A = 16
N_KV_A = 2
GROUP_A = N_Q_A // N_KV_A
N_H_B = 16
BLOCK = 128
WINDOW = 128
assert WINDOW == BLOCK
ROT_DIM = HEAD_DIM // 4
ROPE_THETA = 500000.0
EPS = 1e-6
LANES = 128
FF_CHUNK = 256
ROW_TILE = 512
VMEM_LIMIT = 56 * 1024 * 1024
SB_CUTOFF = -104.0
SB_ROWS = 64
SB_KEYS = 384
SB_GROUP = 11
LOG2E = 1.4426950408889634

_NT = (((1,), (1,)), ((), ()))


def _params(sem):
    return pltpu.CompilerParams(dimension_semantics=sem, vmem_limit_bytes=VMEM_LIMIT)


def _resident(shape):
    return pl.BlockSpec(shape, lambda *_: (0,) * len(shape), pipeline_mode=pl.Buffered(1))


def _rms_mod(x, gain, scale, shift):
    ms = jnp.mean(x * x, axis=-1, keepdims=True)
    y = x * lax.rsqrt(ms + EPS)
    return (y * gain) * (1.0 + scale) + shift


def _split_dot(x, m):
    hi = x.astype(BF16)
    lo = (x - hi.astype(F32)).astype(BF16)
    return (jnp.dot(hi, m, preferred_element_type=F32)
            + jnp.dot(lo, m, preferred_element_type=F32))


def _adaln_kernel(c_ref, w_ref, b_ref, o_ref):
    c = c_ref[...]
    cond = (c * jax.nn.sigmoid(c)).astype(BF16)
    o_ref[0] = jnp.dot(cond, w_ref[0].astype(BF16), preferred_element_type=F32) + b_ref[0]


def _adaln(c, ada_w, ada_b):
    depth, d, n = ada_w.shape
    b = c.shape[0]
    tn = 1024
    return pl.pallas_call(
        _adaln_kernel,
        out_shape=jax.ShapeDtypeStruct((depth, b, n), F32),
        grid=(depth, n // tn),
        in_specs=[
            pl.BlockSpec((b, d), lambda l, j: (0, 0)),
            pl.BlockSpec((1, d, tn), lambda l, j: (l, 0, j)),
            pl.BlockSpec((1, 1, tn), lambda l, j: (l, 0, j)),
        ],
        out_specs=pl.BlockSpec((1, b, tn), lambda l, j: (l, 0, j)),
        compiler_params=_params(("arbitrary", "arbitrary")),
        name="adaln",
    )(c, ada_w, ada_b.reshape(depth, 1, n))


def _rope_consts():
    half = ROT_DIM // 2
    inv_freq = jnp.power(F32(ROPE_THETA), -jnp.arange(half, dtype=F32) * 2.0 / ROT_DIM)
    f = jnp.arange(half)[:, None]
    d = jnp.arange(LANES)[None, :] % HEAD_DIM
    place = jnp.stack([(d == f) | (d == f + half), d == f, d == f + half]).astype(F32)
    place = place * jnp.asarray([1.0, -1.0, 1.0], F32)[:, None, None]
    base = (d >= ROT_DIM).astype(F32)
    return inv_freq.reshape(half, 1), place, base


def _proj_a_kernel(x_ref, mod_ref, g_ref, w_ref, gain_ref, gsum_ref, pos_ref, invf_ref, place_ref,
                   base_ref, q_ref, k_ref, v_ref):
    h = _rms_mod(x_ref[0], g_ref[...], mod_ref[0, 1:2, :], mod_ref[0, 0:1, :]).astype(BF16)
    qkv = jnp.dot(h, w_ref[...], preferred_element_type=F32)
    ang = pos_ref[0].astype(F32) * invf_ref[...]
    cos_t, sin_t = jnp.cos(ang), jnp.sin(ang)
    expand = lambda t, m: lax.dot_general(t, place_ref[m], (((0,), (0,)), ((), ())),
                                          precision=lax.Precision.HIGHEST,
                                          preferred_element_type=F32)
    cos, sup, sdn = expand(cos_t, 0) + base_ref[...], expand(sin_t, 1), expand(sin_t, 2)
    gsum = gsum_ref[...]
    half = ROT_DIM // 2

    def norm_rope(blk, gain):
        ssq = _split_dot(blk * blk, gsum)
        y = blk * lax.rsqrt(ssq * (1.0 / HEAD_DIM) + EPS) * gain
        return (y * cos + pltpu.roll(y, LANES - half, 1) * sup + pltpu.roll(y, half, 1) * sdn)

    nq = q_ref.shape[-1] // LANES
    for cb in range(nq):
        blk = qkv[:, cb * LANES:(cb + 1) * LANES]
        q_ref[0, :, cb * LANES:(cb + 1) * LANES] = norm_rope(blk, gain_ref[0:1, :]).astype(BF16)
    k_ref[0] = norm_rope(qkv[:, nq * LANES:(nq + 1) * LANES], gain_ref[1:2, :]).astype(BF16)
    v_ref[0] = qkv[:, (nq + 1) * LANES:(nq + 2) * LANES].astype(BF16)


def _proj_a(x, mod, gain1, w, gains, gsum, positions):
    b, s, d = x.shape
    tm = ROW_TILE
    nq = N_Q_A * HEAD_DIM
    inv_freq, place, base = _rope_consts()
    row = lambda n: pl.BlockSpec((1, tm, n), lambda bi, i: (bi, i, 0))
    out = lambda n: jax.ShapeDtypeStruct((b, s, n), BF16)
    return pl.pallas_call(
        _proj_a_kernel,
        out_shape=(out(nq), out(LANES), out(LANES)),
        grid=(b, s // tm),
        in_specs=[
            row(d),
            pl.BlockSpec((1, 6, d), lambda bi, i: (bi, 0, 0)),
            _resident((1, d)),
            _resident(w.shape),
            _resident(gains.shape),
            _resident(gsum.shape),
            pl.BlockSpec((1, 1, tm), lambda bi, i: (bi, 0, i)),
            _resident(inv_freq.shape), _resident(place.shape), _resident(base.shape),
        ],
        out_specs=(row(nq), row(LANES), row(LANES)),
        compiler_params=_params(("parallel", "parallel")),
        name="proj_a",
    )(x, mod, gain1, w, gains, gsum, positions.reshape(b, 1, s), inv_freq, place, base)


def _attn_a_kernel(sink_ref, q_ref, kc_ref, kp_ref, vc_ref, vp_ref, o_ref):
    i = pl.program_id(1)
    kcat = jnp.concatenate([kp_ref[0], kc_ref[0]], axis=0)
    vcat = jnp.concatenate([vp_ref[0], vc_ref[0]], axis=0)
    lane2 = lax.broadcasted_iota(jnp.int32, (2 * BLOCK, LANES), 1)
    row2 = lax.broadcasted_iota(jnp.int32, (2 * BLOCK, LANES), 0)
    keep = (lane2 < HEAD_DIM) == (row2 < BLOCK)
    own = lane2 <= (row2 & (BLOCK - 1))
    no_prev = jnp.where(i > 0, 0.0, -1e30).astype(F32)
    rowc = lax.broadcasted_iota(jnp.int32, (2 * BLOCK, 1), 0)
    lane1 = lax.broadcasted_iota(jnp.int32, (BLOCK, LANES), 1)
    scores = []
    for p in range(GROUP_A):
        qp = q_ref[0, :, p * LANES:(p + 1) * LANES]
        qq = jnp.concatenate([qp, qp], axis=0)
        lhs = jnp.where(keep, qq, jnp.zeros_like(qq))
        scores.append(lax.dot_general(lhs, kcat, _NT, preferred_element_type=F32))
    outs = []
    for p in range(GROUP_A):
        s = jnp.where(own, scores[p][:, BLOCK:], scores[p][:, :BLOCK] + no_prev)
        sink = jnp.where(rowc < BLOCK, sink_ref[p], sink_ref[p + GROUP_A])
        m = jnp.maximum(jnp.max(s, axis=-1, keepdims=True), sink)
        e = jnp.exp(s - m)
        denom = jnp.sum(e, axis=-1, keepdims=True) + jnp.exp(sink - m)
        e = e.astype(BF16)
        zero = jnp.zeros_like(e)
        weights = jnp.concatenate([jnp.where(own, zero, e), jnp.where(own, e, zero)], axis=1)
        outs.append((jnp.dot(weights, vcat, preferred_element_type=F32), denom))
    for p, (pv, denom) in enumerate(outs):
        pv = pv * (1.0 / denom)
        o_ref[0, :, p * LANES:(p + 1) * LANES] = jnp.where(
            lane1 < HEAD_DIM, pv[:BLOCK], pv[BLOCK:]).astype(BF16)


def _attn_a(sinks, q, k, v):
    b, s, nq = q.shape
    cur = lambda bi, i: (bi, i, 0)
    prev = lambda bi, i: (bi, jnp.maximum(i - 1, 0), 0)
    return pl.pallas_call(
        _attn_a_kernel,
        out_shape=jax.ShapeDtypeStruct((b, s, nq), BF16),
        grid=(b, s // BLOCK),
        in_specs=[
            pl.BlockSpec(memory_space=pltpu.SMEM),
            pl.BlockSpec((1, BLOCK, nq), cur),
            pl.BlockSpec((1, BLOCK, LANES), cur),
            pl.BlockSpec((1, BLOCK, LANES), prev),
            pl.BlockSpec((1, BLOCK, LANES), cur),
            pl.BlockSpec((1, BLOCK, LANES), prev),
        ],
        out_specs=pl.BlockSpec((1, BLOCK, nq), cur),
        compiler_params=_params(("parallel", "parallel")),
        name="attn_a",
    )(sinks, q, k, k, v, v)


def _proj_b_kernel(x_ref, mod_ref, g_ref, w_ref, o_ref):
    h = _rms_mod(x_ref[0], g_ref[...], mod_ref[0, 1:2, :], mod_ref[0, 0:1, :]).astype(BF16)
    d = w_ref.shape[0]
    for n in range(w_ref.shape[1] // d):
        o_ref[0, :, n * d:(n + 1) * d] = jnp.dot(
            h, w_ref[:, n * d:(n + 1) * d], preferred_element_type=F32).astype(BF16)


def _proj_b(x, mod, gain1, w):
    b, s, d = x.shape
    tm = ROW_TILE
    n = w.shape[1]
    return pl.pallas_call(
        _proj_b_kernel,
        out_shape=jax.ShapeDtypeStruct((b, s, n), BF16),
        grid=(b, s // tm),
        in_specs=[
            pl.BlockSpec((1, tm, d), lambda bi, i: (bi, i, 0)),
            pl.BlockSpec((1, 6, d), lambda bi, i: (bi, 0, 0)),
            _resident((1, d)),
            _resident(w.shape),
        ],
        out_specs=pl.BlockSpec((1, tm, n), lambda bi, i: (bi, i, 0)),
        compiler_params=_params(("parallel", "parallel")),
        name="proj_b",
    )(x, mod, gain1, w)


def _attn_b_kernel(q_ref, k_ref, v_ref, uo_ref, o_ref):
    uo = uo_ref[...]
    nrow = 2 * SB_ROWS
    lane = lax.broadcasted_iota(jnp.int32, (nrow, LANES), 1)
    row = lax.broadcasted_iota(jnp.int32, (nrow, LANES), 0)
    keep = (lane < HEAD_DIM) == (row < SB_ROWS)
    col_minus_row = lane - (row & (SB_ROWS - 1))
    out_head0 = lax.broadcasted_iota(jnp.int32, (SB_ROWS, LANES), 1) < HEAD_DIM
    cutoff = SB_CUTOFF * LOG2E
    sign = jnp.int32(-2 ** 31)

    def stacked_queries(t0):
        q = q_ref[0, pl.ds(t0, SB_ROWS), :]
        qq = jnp.concatenate([q, q], axis=0)
        return jnp.where(keep, qq, jnp.zeros_like(qq))

    def scores(lhs, start, nkeys):
        return lax.dot_general(lhs, k_ref[0, pl.ds(start, nkeys), :], _NT,
                               preferred_element_type=F32) * LOG2E

    def log_terms(z, mask):
        neg_abs = lax.bitcast_convert_type(lax.bitcast_convert_type(z, jnp.int32) | sign, F32)
        log_beta = jnp.minimum(z, 0.0) - jnp.log2(1.0 + jnp.exp2(neg_abs))
        log_rest = log_beta - z
        if mask is not None:
            log_rest = jnp.where(mask, log_rest, 0.0)
        hi = log_rest.astype(BF16)
        lo = (log_rest - hi.astype(F32)).astype(BF16)
        return log_beta, jnp.dot(jnp.concatenate([hi, lo], axis=1), uo,
                                 preferred_element_type=F32)

    def weights(log_beta, sums, later, mask):
        t = log_beta + sums[:, :BLOCK]
        if later is not None:
            t = t + later
        a = jnp.exp2(t)
        if mask is not None:
            a = jnp.where(mask, a, 0.0)
        return a.astype(BF16)

    def store(t0, pv):
        o_ref[0, pl.ds(t0, SB_ROWS), :] = jnp.where(
            out_head0, pv[:SB_ROWS], pv[SB_ROWS:]).astype(BF16)

    def first_windows(units):
        tiles, zs, vals = [], [], []
        for t0, start, limit in units:
            mine = [(b, None if b * BLOCK + BLOCK - 1 < limit else col_minus_row < limit - b * BLOCK)
                    for b in range(SB_KEYS // BLOCK) if b * BLOCK - (SB_ROWS - 1) < limit]
            tiles.append(mine)
            zs.append(scores(stacked_queries(t0), start, len(mine) * BLOCK))
            vals.append(v_ref[0, pl.ds(start, len(mine) * BLOCK), :])
        terms = [dict() for _ in units]
        for b in reversed(range(SB_KEYS // BLOCK)):
            for u, mine in enumerate(tiles):
                for tb, mask in mine:
                    if tb == b:
                        terms[u][b] = log_terms(zs[u][:, b * BLOCK:(b + 1) * BLOCK], mask)
        out = []
        for u, (t0, _, _) in enumerate(units):
            later, ws = None, []
            for b, mask in reversed(tiles[u]):
                log_beta, sums = terms[u][b]
                ws.insert(0, weights(log_beta, sums, later, mask))
                later = sums[:, BLOCK:] if later is None else later + sums[:, BLOCK:]
            pv = jnp.dot(jnp.concatenate(ws, axis=1), vals[u], preferred_element_type=F32)
            store(t0, pv)
            out.append((later, pv))
        return out

    def sweep_earlier(t0, start, total, pv):
        lhs = stacked_queries(t0)

        def cond(st):
            return (st[0] > 0) & (st[1] > 0)

        def body(st):
            end, _, carry, acc = st
            begin = pl.multiple_of(jnp.maximum(end - BLOCK, 0), SB_ROWS)
            mask = lane < end - begin
            log_beta, sums = log_terms(scores(lhs, begin, BLOCK), mask)
            w = weights(log_beta, sums, carry, mask)
            acc = acc + jnp.dot(w, v_ref[0, pl.ds(begin, BLOCK), :], preferred_element_type=F32)
            carry = carry + sums[:, BLOCK:]
            return begin, (jnp.max(carry) > cutoff).astype(jnp.int32), carry, acc

        st = lax.while_loop(cond, body, (jnp.int32(start), jnp.int32(1), total, pv))
        store(t0, st[3])

    def run_units(units):
        res = first_windows(units)
        check = [(u, r) for u, r in zip(units, res) if not (isinstance(u[1], int) and u[1] == 0)]
        if not check:
            return
        flags = [(jnp.max(total) > cutoff).astype(jnp.int32) for _, (total, _) in check]

        @pl.when(functools.reduce(jnp.maximum, flags) > 0)
        def _():
            for ((t0, start, _), (total, pv)), flag in zip(check, flags):
                pl.when(flag > 0)(functools.partial(sweep_earlier, t0, start, total, pv))

    nunit = q_ref.shape[1] // SB_ROWS
    reach = SB_KEYS - SB_ROWS
    nclamp = reach // SB_ROWS
    npeel = nclamp + (nunit - nclamp) % SB_GROUP
    run_units([(u * SB_ROWS, max(u * SB_ROWS - reach, 0), min(u * SB_ROWS, reach))
               for u in range(npeel)])

    def group(g, _):
        base = (npeel + g * SB_GROUP) * SB_ROWS
        t0s = [pl.multiple_of(base + m * SB_ROWS, SB_ROWS) for m in range(SB_GROUP)]
        run_units([(t0, pl.multiple_of(t0 - reach, SB_ROWS), reach) for t0 in t0s])
        return 0

    lax.fori_loop(0, (nunit - npeel) // SB_GROUP, group, 0)


def _attn_b(qkv):
    b, s, n = qkv.shape
    npair = n // 3 // LANES
    assert s % SB_ROWS == 0 and s >= SB_KEYS
    key = jnp.arange(BLOCK)
    later = (key[:, None] > key[None, :]).astype(BF16)
    uo = jnp.concatenate([later, jnp.ones((BLOCK, BLOCK), BF16)], axis=1)
    uo = jnp.concatenate([uo, uo], axis=0)
    blk = lambda off: pl.BlockSpec((1, s, LANES), lambda bi, hp: (bi, 0, off + hp))
    return pl.pallas_call(
        _attn_b_kernel,
        out_shape=jax.ShapeDtypeStruct((b, s, n // 3), BF16),
        grid=(b, npair),
        in_specs=[blk(0), blk(npair), blk(2 * npair), _resident(uo.shape)],
        out_specs=blk(0),
        compiler_params=_params(("parallel", "parallel")),
        name="attn_b",
    )(qkv, qkv, qkv, uo)


def _post_kernel(o_ref, x_ref, mod_ref, g_ref, wo_ref, wg_ref, wu_ref, wd_ref, out_ref, acc_ref):
    y = jnp.dot(o_ref[0], wo_ref[...], preferred_element_type=F32)
    x1 = x_ref[0] + mod_ref[0, 2:3, :] * y
    h = _rms_mod(x1, g_ref[...], mod_ref[0, 4:5, :], mod_ref[0, 3:4, :]).astype(BF16)
    for f in range(wg_ref.shape[0]):
        g = jnp.dot(h, wg_ref[f], preferred_element_type=F32)
        u = jnp.dot(h, wu_ref[f], preferred_element_type=F32)
        act = ((g * jax.nn.sigmoid(g)) * u).astype(BF16)
        down = jnp.dot(act, wd_ref[f], preferred_element_type=F32)
        if f == 0:
            acc_ref[...] = down
        else:
            acc_ref[...] += down
    out_ref[0] = x1 + mod_ref[0, 5:6, :] * acc_ref[...]


def _post(o, x, mod, gain2, wo, wg, wu, wd):
    b, s, d = x.shape
    tm = ROW_TILE
    row = lambda n: pl.BlockSpec((1, tm, n), lambda bi, i: (bi, i, 0))
    return pl.pallas_call(
        _post_kernel,
        out_shape=jax.ShapeDtypeStruct((b, s, d), F32),
        grid=(b, s // tm),
        in_specs=[
            row(o.shape[-1]), row(d),
            pl.BlockSpec((1, 6, d), lambda bi, i: (bi, 0, 0)),
            _resident((1, d)),
            _resident(wo.shape), _resident(wg.shape), _resident(wu.shape), _resident(wd.shape),
        ],
        out_specs=row(d),
        scratch_shapes=[pltpu.VMEM((tm, d), F32)],
        compiler_params=_params(("parallel", "parallel")),
        name="post",
    )(o, x, mod, gain2, wo, wg, wu, wd)


def _pair_perm():
    cols = []
    for p in range(GROUP_A):
        for h in (p, p + GROUP_A):
            cols.extend(range(h * HEAD_DIM, (h + 1) * HEAD_DIM))
    return jnp.asarray(cols, jnp.int32)


def kernel(x, c, positions, ada_w, ada_b, norm1_g, norm2_g, wqkv_a, q_norm_a, k_norm_a, sinks_a,
           wo_a, wqkv_b, wo_b, w_gate, w_up, w_down):
    b, s, d = x.shape
    depth = ada_w.shape[0]
    d_ff = w_gate.shape[-1]
    nf = d_ff // FF_CHUNK
    scale = HEAD_DIM ** -0.5

    mod = _adaln(c, ada_w, ada_b).reshape(depth, b, 6, d)

    perm = _pair_perm()
    nqa = N_Q_A * HEAD_DIM
    lane = jnp.arange(LANES)
    gsum = (lane[:, None] // HEAD_DIM == lane[None, :] // HEAD_DIM).astype(BF16)

    for i in range(depth):
        j = i // 2
        if i % 2 == 0:
            w = wqkv_a[j]
            w = jnp.concatenate([w[:, :nqa][:, perm], w[:, nqa:]], axis=1).astype(BF16)
            gains = jnp.stack([jnp.tile(q_norm_a[j] * scale, 2), jnp.tile(k_norm_a[j], 2)])
            q, k, v = _proj_a(x, mod[i], norm1_g[i].reshape(1, d), w, gains, gsum, positions)
            o = _attn_a(sinks_a[j], q, k, v)
            wo = wo_a[j][perm, :].astype(BF16)
        else:
            w = wqkv_b[j]
            nqb = N_H_B * HEAD_DIM
            w = jnp.concatenate([w[:, :nqb] * scale, w[:, nqb:]], axis=1).astype(BF16)
            o = _attn_b(_proj_b(x, mod[i], norm1_g[i].reshape(1, d), w))
            wo = wo_b[j].astype(BF16)
        wg = w_gate[i].astype(BF16).reshape(d, nf, FF_CHUNK).transpose(1, 0, 2)
        wu = w_up[i].astype(BF16).reshape(d, nf, FF_CHUNK).transpose(1, 0, 2)
        wd = w_down[i].astype(BF16).reshape(nf, FF_CHUNK, d)
        x = _post(o, x, mod[i], norm2_g[i].reshape(1, d), wo, wg, wu, wd)
    return x
```

```python
import functools

import jax
import jax.numpy as jnp
from jax import lax
from jax.experimental import pallas as pl
from jax.experimental.pallas import tpu as pltpu

F32 = jnp.float32
BF16 = jnp.bfloat16

HEAD_DIM = 64
N_Q_A = 16
N_KV_A = 2
GROUP_A = N_Q_A // N_KV_A
N_H_B = 16
BLOCK = 128
WINDOW = 128
assert WINDOW == BLOCK
ROT_DIM = HEAD_DIM // 4
ROPE_THETA = 500000.0
EPS = 1e-6
LANES = 128
FF_CHUNK = 256
ROW_TILE = 512
PROJ_A_ROWS = 1024
PROJ_A_SUB = 256
PROJ_B_SUB = 256
ATTN_A_BLOCKS = 4
VMEM_LIMIT = 56 * 1024 * 1024
SB_CUTOFF = -104.0
SB_ROWS = 64
SB_KEYS = 384
SB_GROUP = 11
LOG2E = 1.4426950408889634

_NT = (((1,), (1,)), ((), ()))


def _params(sem):
    return pltpu.CompilerParams(dimension_semantics=sem, vmem_limit_bytes=VMEM_LIMIT)


def _resident(shape):
    return pl.BlockSpec(shape, lambda *_: (0,) * len(shape), pipeline_mode=pl.Buffered(1))


def _rms_mod(x, gain, scale, shift):
    ms = jnp.mean(x * x, axis=-1, keepdims=True)
    y = x * lax.rsqrt(ms + EPS)
    return (y * gain) * (1.0 + scale) + shift


def _split_dot(x, m):
    hi = x.astype(BF16)
    lo = (x - hi.astype(F32)).astype(BF16)
    return (jnp.dot(hi, m, preferred_element_type=F32)
            + jnp.dot(lo, m, preferred_element_type=F32))


def _adaln_kernel(c_ref, w_ref, b_ref, o_ref):
    c = c_ref[...]
    cond = (c * jax.nn.sigmoid(c)).astype(BF16)
    o_ref[0] = jnp.dot(cond, w_ref[0].astype(BF16), preferred_element_type=F32) + b_ref[0]


def _adaln(c, ada_w, ada_b):
    depth, d, n = ada_w.shape
    b = c.shape[0]
    tn = 1024
    return pl.pallas_call(
        _adaln_kernel,
        out_shape=jax.ShapeDtypeStruct((depth, b, n), F32),
        grid=(depth, n // tn),
        in_specs=[
            pl.BlockSpec((b, d), lambda l, j: (0, 0)),
            pl.BlockSpec((1, d, tn), lambda l, j: (l, 0, j)),
            pl.BlockSpec((1, 1, tn), lambda l, j: (l, 0, j)),
        ],
        out_specs=pl.BlockSpec((1, b, tn), lambda l, j: (l, 0, j)),
        compiler_params=_params(("arbitrary", "arbitrary")),
        name="adaln",
    )(c, ada_w, ada_b.reshape(depth, 1, n))


def _rope_consts():
    half = ROT_DIM // 2
    inv_freq = jnp.power(F32(ROPE_THETA), -jnp.arange(half, dtype=F32) * 2.0 / ROT_DIM)
    f = jnp.arange(half)[:, None]
    d = jnp.arange(LANES)[None, :] % HEAD_DIM
    zero = jnp.zeros((half, LANES), F32)
    from_cos = jnp.concatenate([((d == f) | (d == f + half)).astype(F32), zero, zero], axis=1)
    from_sin = jnp.concatenate([zero, -(d == f).astype(F32), (d == f + half).astype(F32)], axis=1)
    place = jnp.concatenate([from_cos] * 3 + [from_sin] * 3, axis=0).astype(BF16)
    base = (d >= ROT_DIM).astype(F32)
    return inv_freq.reshape(half, 1), place, base


def _split3(x):
    hi = x.astype(BF16).astype(F32)
    mid = (x - hi).astype(BF16).astype(F32)
    return [hi, mid, x - hi - mid]


def _proj_a_kernel(x_ref, mod_ref, g_ref, w_ref, gain_ref, gsum_ref, pos_ref, invf_ref, place_ref,
                   base_ref, q_ref, k_ref, v_ref):
    ang = pos_ref[0].astype(F32) * invf_ref[...]
    trig = jnp.concatenate(_split3(jnp.cos(ang)) + _split3(jnp.sin(ang)), axis=0).astype(BF16)
    gsum = gsum_ref[...]
    half = ROT_DIM // 2
    nq = q_ref.shape[-1] // LANES

    def project(rows):
        h = _rms_mod(x_ref[0, rows, :], g_ref[...], mod_ref[0, 1:2, :], mod_ref[0, 0:1, :])
        return jnp.dot(h.astype(BF16), w_ref[...], preferred_element_type=F32)

    def finish(rows, qkv):
        tables = lax.dot_general(trig[:, rows], place_ref[...], (((0,), (0,)), ((), ())),
                                 preferred_element_type=F32)
        cos = tables[:, :LANES] + base_ref[...]
        sup, sdn = tables[:, LANES:2 * LANES], tables[:, 2 * LANES:]

        def norm_rope(blk, gain):
            ssq = _split_dot(blk * blk, gsum)
            y = blk * lax.rsqrt(ssq * (1.0 / HEAD_DIM) + EPS) * gain
            return y * cos + pltpu.roll(y, LANES - half, 1) * sup + pltpu.roll(y, half, 1) * sdn

        for cb in range(nq):
            blk = qkv[:, cb * LANES:(cb + 1) * LANES]
            q_ref[0, rows, cb * LANES:(cb + 1) * LANES] = norm_rope(
                blk, gain_ref[0:1, :]).astype(BF16)
        k_ref[0, rows, :] = norm_rope(qkv[:, nq * LANES:(nq + 1) * LANES],
                                      gain_ref[1:2, :]).astype(BF16)
        v_ref[0, rows, :] = qkv[:, (nq + 1) * LANES:(nq + 2) * LANES].astype(BF16)

    subs = [slice(r, r + PROJ_A_SUB) for r in range(0, x_ref.shape[1], PROJ_A_SUB)]
    pending = None
    for rows in subs:
        qkv = project(rows)
        if pending is not None:
            finish(*pending)
        pending = (rows, qkv)
    finish(*pending)


def _proj_a(x, mod, gain1, w, gains, gsum, positions):
    b, s, d = x.shape
    tm = PROJ_A_ROWS
    nq = N_Q_A * HEAD_DIM
    inv_freq, place, base = _rope_consts()
    row = lambda n: pl.BlockSpec((1, tm, n), lambda bi, i: (bi, i, 0))
    out = lambda n: jax.ShapeDtypeStruct((b, s, n), BF16)
    return pl.pallas_call(
        _proj_a_kernel,
        out_shape=(out(nq), out(LANES), out(LANES)),
        grid=(b, s // tm),
        in_specs=[
            row(d),
            pl.BlockSpec((1, 6, d), lambda bi, i: (bi, 0, 0)),
            _resident((1, d)),
            _resident(w.shape),
            _resident(gains.shape),
            _resident(gsum.shape),
            pl.BlockSpec((1, 1, tm), lambda bi, i: (bi, 0, i)),
            _resident(inv_freq.shape), _resident(place.shape), _resident(base.shape),
        ],
        out_specs=(row(nq), row(LANES), row(LANES)),
        compiler_params=_params(("parallel", "parallel")),
        name="proj_a",
    )(x, mod, gain1, w, gains, gsum, positions.reshape(b, 1, s), inv_freq, place, base)


def _attn_a_kernel(sink_ref, q_ref, kc_ref, kp_ref, vc_ref, vp_ref, o_ref):
    first = pl.program_id(1) == 0
    lane2 = lax.broadcasted_iota(jnp.int32, (2 * BLOCK, LANES), 1)
    row2 = lax.broadcasted_iota(jnp.int32, (2 * BLOCK, LANES), 0)
    keep = (lane2 < HEAD_DIM) == (row2 < BLOCK)
    own = lane2 <= (row2 & (BLOCK - 1))
    rowc = lax.broadcasted_iota(jnp.int32, (2 * BLOCK, 1), 0)
    lane1 = lax.broadcasted_iota(jnp.int32, (BLOCK, LANES), 1)
    sinks = [jnp.where(rowc < BLOCK, sink_ref[p], sink_ref[p + GROUP_A]) for p in range(GROUP_A)]

    def keys_values(jb):
        rows = slice(jb * BLOCK, (jb + 1) * BLOCK)
        before = slice((jb - 1) * BLOCK, jb * BLOCK)
        kprev, vprev = (kp_ref[0], vp_ref[0]) if jb == 0 else (kc_ref[0, before, :], vc_ref[0, before, :])
        return (jnp.concatenate([kprev, kc_ref[0, rows, :]], axis=0),
                jnp.concatenate([vprev, vc_ref[0, rows, :]], axis=0))

    def all_scores(jb, kcat):
        out = []
        for p in range(GROUP_A):
            qp = q_ref[0, jb * BLOCK:(jb + 1) * BLOCK, p * LANES:(p + 1) * LANES]
            qq = jnp.concatenate([qp, qp], axis=0)
            lhs = jnp.where(keep, qq, jnp.zeros_like(qq))
            out.append(lax.dot_general(lhs, kcat, _NT, preferred_element_type=F32))
        return out

    nblk = q_ref.shape[1] // BLOCK
    kv = keys_values(0)
    scores = all_scores(0, kv[0])
    for jb in range(nblk):
        vcat = kv[1]
        no_prev = jnp.where(first, -1e30, 0.0).astype(F32) if jb == 0 else None
        weights, denoms = [], []
        for p in range(GROUP_A):
            before = scores[p][:, :BLOCK]
            s = jnp.where(own, scores[p][:, BLOCK:], before if no_prev is None else before + no_prev)
            m = jnp.maximum(jnp.max(s, axis=-1, keepdims=True), sinks[p])
            e = jnp.exp(s - m)
            denoms.append(jnp.sum(e, axis=-1, keepdims=True) + jnp.exp(sinks[p] - m))
            e = e.astype(BF16)
            zero = jnp.zeros_like(e)
            weights.append(jnp.concatenate([jnp.where(own, zero, e), jnp.where(own, e, zero)], axis=1))
        if jb + 1 < nblk:
            kv = keys_values(jb + 1)
            scores = all_scores(jb + 1, kv[0])
        for p in range(GROUP_A):
            pv = jnp.dot(weights[p], vcat, preferred_element_type=F32) * (1.0 / denoms[p])
            o_ref[0, jb * BLOCK:(jb + 1) * BLOCK, p * LANES:(p + 1) * LANES] = jnp.where(
                lane1 < HEAD_DIM, pv[:BLOCK], pv[BLOCK:]).astype(BF16)


def _attn_a(sinks, q, k, v):
    b, s, nq = q.shape
    rows = ATTN_A_BLOCKS * BLOCK
    cur = lambda bi, i: (bi, i, 0)
    prev = lambda bi, i: (bi, jnp.maximum(i * ATTN_A_BLOCKS - 1, 0), 0)
    return pl.pallas_call(
        _attn_a_kernel,
        out_shape=jax.ShapeDtypeStruct((b, s, nq), BF16),
        grid=(b, s // rows),
        in_specs=[
            pl.BlockSpec(memory_space=pltpu.SMEM),
            pl.BlockSpec((1, rows, nq), cur),
            pl.BlockSpec((1, rows, LANES), cur),
            pl.BlockSpec((1, BLOCK, LANES), prev),
            pl.BlockSpec((1, rows, LANES), cur),
            pl.BlockSpec((1, BLOCK, LANES), prev),
        ],
        out_specs=pl.BlockSpec((1, rows, nq), cur),
        compiler_params=_params(("parallel", "parallel")),
        name="attn_a",
    )(sinks, q, k, k, v, v)


def _proj_b_kernel(x_ref, mod_ref, g_ref, w_ref, o_ref):
    d = w_ref.shape[0]
    for r in range(0, x_ref.shape[1], PROJ_B_SUB):
        rows = slice(r, r + PROJ_B_SUB)
        h = _rms_mod(x_ref[0, rows, :], g_ref[...], mod_ref[0, 1:2, :],
                     mod_ref[0, 0:1, :]).astype(BF16)
        for n in range(w_ref.shape[1] // d):
            o_ref[0, rows, n * d:(n + 1) * d] = jnp.dot(
                h, w_ref[:, n * d:(n + 1) * d], preferred_element_type=F32).astype(BF16)


def _proj_b(x, mod, gain1, w):
    b, s, d = x.shape
    tm = ROW_TILE
    n = w.shape[1]
    return pl.pallas_call(
        _proj_b_kernel,
        out_shape=jax.ShapeDtypeStruct((b, s, n), BF16),
        grid=(b, s // tm),
        in_specs=[
            pl.BlockSpec((1, tm, d), lambda bi, i: (bi, i, 0)),
            pl.BlockSpec((1, 6, d), lambda bi, i: (bi, 0, 0)),
            _resident((1, d)),
            _resident(w.shape),
        ],
        out_specs=pl.BlockSpec((1, tm, n), lambda bi, i: (bi, i, 0)),
        compiler_params=_params(("parallel", "parallel")),
        name="proj_b",
    )(x, mod, gain1, w)


def _attn_b_kernel(q_ref, k_ref, v_ref, uo_ref, o_ref):
    uo = uo_ref[...]
    nrow = 2 * SB_ROWS
    lane = lax.broadcasted_iota(jnp.int32, (nrow, LANES), 1)
    row = lax.broadcasted_iota(jnp.int32, (nrow, LANES), 0)
    keep = (lane < HEAD_DIM) == (row < SB_ROWS)
    col_minus_row = lane - (row & (SB_ROWS - 1))
    out_head0 = lax.broadcasted_iota(jnp.int32, (SB_ROWS, LANES), 1) < HEAD_DIM
    cutoff = SB_CUTOFF * LOG2E
    sign = jnp.int32(-2 ** 31)

    def stacked_queries(t0):
        q = q_ref[0, pl.ds(t0, SB_ROWS), :]
        qq = jnp.concatenate([q, q], axis=0)
        return jnp.where(keep, qq, jnp.zeros_like(qq))

    def scores(lhs, start, nkeys):
        return lax.dot_general(lhs, k_ref[0, pl.ds(start, nkeys), :], _NT,
                               preferred_element_type=F32) * LOG2E

    def log_terms(z, mask):
        neg_abs = lax.bitcast_convert_type(lax.bitcast_convert_type(z, jnp.int32) | sign, F32)
        log_beta = jnp.minimum(z, 0.0) - jnp.log2(1.0 + jnp.exp2(neg_abs))
        log_rest = log_beta - z
        if mask is not None:
            log_rest = jnp.where(mask, log_rest, 0.0)
        hi = log_rest.astype(BF16)
        lo = (log_rest - hi.astype(F32)).astype(BF16)
        return log_beta, jnp.dot(jnp.concatenate([hi, lo], axis=1), uo,
                                 preferred_element_type=F32)

    def weights(log_beta, sums, later, mask):
        t = log_beta + sums[:, :BLOCK]
        if later is not None:
            t = t + later
        a = jnp.exp2(t)
        if mask is not None:
            a = jnp.where(mask, a, 0.0)
        return a.astype(BF16)

    def store(t0, pv):
        o_ref[0, pl.ds(t0, SB_ROWS), :] = jnp.where(
            out_head0, pv[:SB_ROWS], pv[SB_ROWS:]).astype(BF16)

    def first_windows(units):
        tiles, zs, vals = [], [], []
        for t0, start, limit in units:
            mine = [(b, None if b * BLOCK + BLOCK - 1 < limit else col_minus_row < limit - b * BLOCK)
                    for b in range(SB_KEYS // BLOCK) if b * BLOCK - (SB_ROWS - 1) < limit]
            tiles.append(mine)
            zs.append(scores(stacked_queries(t0), start, len(mine) * BLOCK))
            vals.append(v_ref[0, pl.ds(start, len(mine) * BLOCK), :])
        terms = [dict() for _ in units]
        for b in reversed(range(SB_KEYS // BLOCK)):
            for u, mine in enumerate(tiles):
                for tb, mask in mine:
                    if tb == b:
                        terms[u][b] = log_terms(zs[u][:, b * BLOCK:(b + 1) * BLOCK], mask)
        out = []
        for u, (t0, _, _) in enumerate(units):
            later, ws = None, []
            for b, mask in reversed(tiles[u]):
                log_beta, sums = terms[u][b]
                ws.insert(0, weights(log_beta, sums, later, mask))
                later = sums[:, BLOCK:] if later is None else later + sums[:, BLOCK:]
            pv = jnp.dot(jnp.concatenate(ws, axis=1), vals[u], preferred_element_type=F32)
            store(t0, pv)
            out.append((later, pv))
        return out

    def sweep_earlier(t0, start, total, pv):
        lhs = stacked_queries(t0)

        def cond(st):
            return (st[0] > 0) & (st[1] > 0)

        def body(st):
            end, _, carry, acc = st
            begin = pl.multiple_of(jnp.maximum(end - BLOCK, 0), SB_ROWS)
            mask = lane < end - begin
            log_beta, sums = log_terms(scores(lhs, begin, BLOCK), mask)
            w = weights(log_beta, sums, carry, mask)
            acc = acc + jnp.dot(w, v_ref[0, pl.ds(begin, BLOCK), :], preferred_element_type=F32)
            carry = carry + sums[:, BLOCK:]
            return begin, (jnp.max(carry) > cutoff).astype(jnp.int32), carry, acc

        st = lax.while_loop(cond, body, (jnp.int32(start), jnp.int32(1), total, pv))
        store(t0, st[3])

    def run_units(units):
        res = first_windows(units)
        check = [(u, r) for u, r in zip(units, res) if not (isinstance(u[1], int) and u[1] == 0)]
        if not check:
            return
        flags = [(jnp.max(total) > cutoff).astype(jnp.int32) for _, (total, _) in check]

        @pl.when(functools.reduce(jnp.maximum, flags) > 0)
        def _():
            for ((t0, start, _), (total, pv)), flag in zip(check, flags):
                pl.when(flag > 0)(functools.partial(sweep_earlier, t0, start, total, pv))

    nunit = q_ref.shape[1] // SB_ROWS
    reach = SB_KEYS - SB_ROWS
    nclamp = reach // SB_ROWS
    npeel = nclamp + (nunit - nclamp) % SB_GROUP
    run_units([(u * SB_ROWS, max(u * SB_ROWS - reach, 0), min(u * SB_ROWS, reach))
               for u in range(npeel)])

    def group(g, _):
        base = (npeel + g * SB_GROUP) * SB_ROWS
        t0s = [pl.multiple_of(base + m * SB_ROWS, SB_ROWS) for m in range(SB_GROUP)]
        run_units([(t0, pl.multiple_of(t0 - reach, SB_ROWS), reach) for t0 in t0s])
        return 0

    lax.fori_loop(0, (nunit - npeel) // SB_GROUP, group, 0)


def _attn_b(qkv):
    b, s, n = qkv.shape
    npair = n // 3 // LANES
    assert s % SB_ROWS == 0 and s >= SB_KEYS
    key = jnp.arange(BLOCK)
    later = (key[:, None] > key[None, :]).astype(BF16)
    uo = jnp.concatenate([later, jnp.ones((BLOCK, BLOCK), BF16)], axis=1)
    uo = jnp.concatenate([uo, uo], axis=0)
    blk = lambda off: pl.BlockSpec((1, s, LANES), lambda bi, hp: (bi, 0, off + hp))
    return pl.pallas_call(
        _attn_b_kernel,
        out_shape=jax.ShapeDtypeStruct((b, s, n // 3), BF16),
        grid=(b, npair),
        in_specs=[blk(0), blk(npair), blk(2 * npair), _resident(uo.shape)],
        out_specs=blk(0),
        compiler_params=_params(("parallel", "parallel")),
        name="attn_b",
    )(qkv, qkv, qkv, uo)


def _post_kernel(o_ref, x_ref, mod_ref, g_ref, wo_ref, wg_ref, wu_ref, wd_ref, out_ref, acc_ref):
    y = jnp.dot(o_ref[0], wo_ref[...], preferred_element_type=F32)
    x1 = x_ref[0] + mod_ref[0, 2:3, :] * y
    h = _rms_mod(x1, g_ref[...], mod_ref[0, 4:5, :], mod_ref[0, 3:4, :]).astype(BF16)
    for f in range(wg_ref.shape[1] // FF_CHUNK):
        cols = slice(f * FF_CHUNK, (f + 1) * FF_CHUNK)
        g = jnp.dot(h, wg_ref[:, cols], preferred_element_type=F32)
        u = jnp.dot(h, wu_ref[:, cols], preferred_element_type=F32)
        act = ((g * jax.nn.sigmoid(g)) * u).astype(BF16)
        down = jnp.dot(act, wd_ref[cols, :], preferred_element_type=F32)
        if f == 0:
            acc_ref[...] = down
        else:
            acc_ref[...] += down
    out_ref[0] = x1 + mod_ref[0, 5:6, :] * acc_ref[...]


def _post(o, x, mod, gain2, wo, wg, wu, wd):
    b, s, d = x.shape
    tm = ROW_TILE
    row = lambda n: pl.BlockSpec((1, tm, n), lambda bi, i: (bi, i, 0))
    return pl.pallas_call(
        _post_kernel,
        out_shape=jax.ShapeDtypeStruct((b, s, d), F32),
        grid=(b, s // tm),
        in_specs=[
            row(o.shape[-1]), row(d),
            pl.BlockSpec((1, 6, d), lambda bi, i: (bi, 0, 0)),
            _resident((1, d)),
            _resident(wo.shape), _resident(wg.shape), _resident(wu.shape), _resident(wd.shape),
        ],
        out_specs=row(d),
        scratch_shapes=[pltpu.VMEM((tm, d), F32)],
        compiler_params=_params(("parallel", "parallel")),
        name="post",
    )(o, x, mod, gain2, wo, wg, wu, wd)


def _pair_perm():
    cols = []
    for p in range(GROUP_A):
        for h in (p, p + GROUP_A):
            cols.extend(range(h * HEAD_DIM, (h + 1) * HEAD_DIM))
    return jnp.asarray(cols, jnp.int32)


def kernel(x, c, positions, ada_w, ada_b, norm1_g, norm2_g, wqkv_a, q_norm_a, k_norm_a, sinks_a,
           wo_a, wqkv_b, wo_b, w_gate, w_up, w_down):
    b, s, d = x.shape
    depth = ada_w.shape[0]
    assert w_gate.shape[-1] % FF_CHUNK == 0
    scale = HEAD_DIM ** -0.5

    mod = _adaln(c, ada_w, ada_b).reshape(depth, b, 6, d)

    perm = _pair_perm()
    nqa = N_Q_A * HEAD_DIM
    lane = jnp.arange(LANES)
    gsum = (lane[:, None] // HEAD_DIM == lane[None, :] // HEAD_DIM).astype(BF16)

    for i in range(depth):
        j = i // 2
        if i % 2 == 0:
            w = wqkv_a[j]
            w = jnp.concatenate([w[:, :nqa][:, perm], w[:, nqa:]], axis=1).astype(BF16)
            gains = jnp.stack([jnp.tile(q_norm_a[j] * scale, 2), jnp.tile(k_norm_a[j], 2)])
            q, k, v = _proj_a(x, mod[i], norm1_g[i].reshape(1, d), w, gains, gsum, positions)
            o = _attn_a(sinks_a[j], q, k, v)
            wo = wo_a[j][perm, :].astype(BF16)
        else:
            w = wqkv_b[j]
            nqb = N_H_B * HEAD_DIM
            w = jnp.concatenate([w[:, :nqb] * scale, w[:, nqb:]], axis=1).astype(BF16)
            o = _attn_b(_proj_b(x, mod[i], norm1_g[i].reshape(1, d), w))
            wo = wo_b[j].astype(BF16)
        x = _post(o, x, mod[i], norm2_g[i].reshape(1, d), wo, w_gate[i].astype(BF16),
                  w_up[i].astype(BF16), w_down[i].astype(BF16))
    return x
```

```python
import functools

import jax
import jax.numpy as jnp
from jax import lax
from jax.experimental import pallas as pl
from jax.experimental.pallas import tpu as pltpu

F32 = jnp.float32
BF16 = jnp.bfloat16

HEAD_DIM = 64
N_Q_A = 16
N_KV_A = 2
GROUP_A = N_Q_A // N_KV_A
N_H_B = 16
BLOCK = 128
WINDOW = 128
assert WINDOW == BLOCK
ROT_DIM = HEAD_DIM // 4
ROPE_THETA = 500000.0
EPS = 1e-6
LANES = 128
FF_CHUNK = 256
ROW_TILE = 512
PROJ_A_ROWS = 1024
PROJ_A_SUB = 256
PROJ_B_SUB = 256
ATTN_A_BLOCKS = 4
LAYER_A_RATIO = 4
VMEM_LIMIT = 56 * 1024 * 1024
SB_CUTOFF = -104.0
SB_ROWS = 64
SB_KEYS = 384
SB_GROUP = 11
LOG2E = 1.4426950408889634

_NT = (((1,), (1,)), ((), ()))


def _params(sem):
    return pltpu.CompilerParams(dimension_semantics=sem, vmem_limit_bytes=VMEM_LIMIT)


def _resident(shape):
    return pl.BlockSpec(shape, lambda *_: (0,) * len(shape), pipeline_mode=pl.Buffered(1))


def _rms_mod(x, gain, scale, shift):
    ms = jnp.mean(x * x, axis=-1, keepdims=True)
    y = x * lax.rsqrt(ms + EPS)
    return (y * gain) * (1.0 + scale) + shift


def _split_dot(x, m):
    hi = x.astype(BF16)
    lo = (x - hi.astype(F32)).astype(BF16)
    return (jnp.dot(hi, m, preferred_element_type=F32)
            + jnp.dot(lo, m, preferred_element_type=F32))


def _adaln_kernel(c_ref, w_ref, b_ref, o_ref):
    c = c_ref[...]
    cond = (c * jax.nn.sigmoid(c)).astype(BF16)
    o_ref[0] = jnp.dot(cond, w_ref[0].astype(BF16), preferred_element_type=F32) + b_ref[0]


def _adaln(c, ada_w, ada_b):
    depth, d, n = ada_w.shape
    b = c.shape[0]
    tn = 1024
    return pl.pallas_call(
        _adaln_kernel,
        out_shape=jax.ShapeDtypeStruct((depth, b, n), F32),
        grid=(depth, n // tn),
        in_specs=[
            pl.BlockSpec((b, d), lambda l, j: (0, 0)),
            pl.BlockSpec((1, d, tn), lambda l, j: (l, 0, j)),
            pl.BlockSpec((1, 1, tn), lambda l, j: (l, 0, j)),
        ],
        out_specs=pl.BlockSpec((1, b, tn), lambda l, j: (l, 0, j)),
        compiler_params=_params(("arbitrary", "arbitrary")),
        name="adaln",
    )(c, ada_w, ada_b.reshape(depth, 1, n))


def _rope_consts():
    half = ROT_DIM // 2
    inv_freq = jnp.power(F32(ROPE_THETA), -jnp.arange(half, dtype=F32) * 2.0 / ROT_DIM)
    f = jnp.arange(half)[:, None]
    d = jnp.arange(LANES)[None, :] % HEAD_DIM
    zero = jnp.zeros((half, LANES), F32)
    from_cos = jnp.concatenate([((d == f) | (d == f + half)).astype(F32), zero, zero], axis=1)
    from_sin = jnp.concatenate([zero, -(d == f).astype(F32), (d == f + half).astype(F32)], axis=1)
    place = jnp.concatenate([from_cos] * 3 + [from_sin] * 3, axis=0).astype(BF16)
    base = (d >= ROT_DIM).astype(F32)
    return inv_freq.reshape(half, 1), place, base


def _split3(x):
    hi = x.astype(BF16).astype(F32)
    mid = (x - hi).astype(BF16).astype(F32)
    return [hi, mid, x - hi - mid]


def _proj_a_kernel(x_ref, mod_ref, g_ref, w_ref, gain_ref, gsum_ref, pos_ref, invf_ref, place_ref,
                   base_ref, q_ref, k_ref, v_ref):
    ang = pos_ref[0].astype(F32) * invf_ref[...]
    trig = jnp.concatenate(_split3(jnp.cos(ang)) + _split3(jnp.sin(ang)), axis=0).astype(BF16)
    gsum = gsum_ref[...]
    half = ROT_DIM // 2
    nq = q_ref.shape[-1] // LANES

    def project(rows):
        h = _rms_mod(x_ref[0, rows, :], g_ref[...], mod_ref[0, 1:2, :], mod_ref[0, 0:1, :])
        return jnp.dot(h.astype(BF16), w_ref[...], preferred_element_type=F32)

    def finish(rows, qkv):
        tables = lax.dot_general(trig[:, rows], place_ref[...], (((0,), (0,)), ((), ())),
                                 preferred_element_type=F32)
        cos = tables[:, :LANES] + base_ref[...]
        sup, sdn = tables[:, LANES:2 * LANES], tables[:, 2 * LANES:]

        def norm_rope(blk, gain):
            ssq = _split_dot(blk * blk, gsum)
            y = blk * lax.rsqrt(ssq * (1.0 / HEAD_DIM) + EPS) * gain
            return y * cos + pltpu.roll(y, LANES - half, 1) * sup + pltpu.roll(y, half, 1) * sdn

        for cb in range(nq):
            blk = qkv[:, cb * LANES:(cb + 1) * LANES]
            q_ref[0, rows, cb * LANES:(cb + 1) * LANES] = norm_rope(
                blk, gain_ref[0:1, :]).astype(BF16)
        k_ref[0, rows, :] = norm_rope(qkv[:, nq * LANES:(nq + 1) * LANES],
                                      gain_ref[1:2, :]).astype(BF16)
        v_ref[0, rows, :] = qkv[:, (nq + 1) * LANES:(nq + 2) * LANES].astype(BF16)

    subs = [slice(r, r + PROJ_A_SUB) for r in range(0, x_ref.shape[1], PROJ_A_SUB)]
    pending = None
    for rows in subs:
        qkv = project(rows)
        if pending is not None:
            finish(*pending)
        pending = (rows, qkv)
    finish(*pending)


def _proj_a(x, mod, gain1, w, gains, gsum, positions):
    b, s, d = x.shape
    tm = PROJ_A_ROWS
    nq = N_Q_A * HEAD_DIM
    inv_freq, place, base = _rope_consts()
    row = lambda n: pl.BlockSpec((1, tm, n), lambda bi, i: (bi, i, 0))
    out = lambda n: jax.ShapeDtypeStruct((b, s, n), BF16)
    return pl.pallas_call(
        _proj_a_kernel,
        out_shape=(out(nq), out(LANES), out(LANES)),
        grid=(b, s // tm),
        in_specs=[
            row(d),
            pl.BlockSpec((1, 6, d), lambda bi, i: (bi, 0, 0)),
            _resident((1, d)),
            _resident(w.shape),
            _resident(gains.shape),
            _resident(gsum.shape),
            pl.BlockSpec((1, 1, tm), lambda bi, i: (bi, 0, i)),
            _resident(inv_freq.shape), _resident(place.shape), _resident(base.shape),
        ],
        out_specs=(row(nq), row(LANES), row(LANES)),
        compiler_params=_params(("parallel", "parallel")),
        name="proj_a",
    )(x, mod, gain1, w, gains, gsum, positions.reshape(b, 1, s), inv_freq, place, base)


def _attn_a_steps(first, sink_ref, q_ref, kc_ref, kp_ref, vc_ref, vp_ref, store):
    lane2 = lax.broadcasted_iota(jnp.int32, (2 * BLOCK, LANES), 1)
    row2 = lax.broadcasted_iota(jnp.int32, (2 * BLOCK, LANES), 0)
    keep = (lane2 < HEAD_DIM) == (row2 < BLOCK)
    own = lane2 <= (row2 & (BLOCK - 1))
    rowc = lax.broadcasted_iota(jnp.int32, (2 * BLOCK, 1), 0)
    lane1 = lax.broadcasted_iota(jnp.int32, (BLOCK, LANES), 1)
    sinks = [jnp.where(rowc < BLOCK, sink_ref[p], sink_ref[p + GROUP_A]) for p in range(GROUP_A)]

    def keys_values(jb):
        rows = slice(jb * BLOCK, (jb + 1) * BLOCK)
        before = slice((jb - 1) * BLOCK, jb * BLOCK)
        kprev, vprev = (kp_ref[0], vp_ref[0]) if jb == 0 else (kc_ref[0, before, :], vc_ref[0, before, :])
        return (jnp.concatenate([kprev, kc_ref[0, rows, :]], axis=0),
                jnp.concatenate([vprev, vc_ref[0, rows, :]], axis=0))

    def all_scores(jb, kcat):
        lhs = []
        for p in range(GROUP_A):
            qp = q_ref[0, jb * BLOCK:(jb + 1) * BLOCK, p * LANES:(p + 1) * LANES]
            qq = jnp.concatenate([qp, qp], axis=0)
            lhs.append(jnp.where(keep, qq, jnp.zeros_like(qq)))
        s = lax.dot_general(jnp.concatenate(lhs, axis=0), kcat, _NT, preferred_element_type=F32)
        return [s[p * 2 * BLOCK:(p + 1) * 2 * BLOCK] for p in range(GROUP_A)]

    nblk = q_ref.shape[1] // BLOCK
    kv = keys_values(0)
    scores = all_scores(0, kv[0])
    yield
    for jb in range(nblk):
        vcat = kv[1]
        no_prev = jnp.where(first, -1e30, 0.0).astype(F32) if jb == 0 else None
        weights, denoms = [], []
        for p in range(GROUP_A):
            before = scores[p][:, :BLOCK]
            s = jnp.where(own, scores[p][:, BLOCK:], before if no_prev is None else before + no_prev)
            m = jnp.maximum(jnp.max(s, axis=-1, keepdims=True), sinks[p])
            e = jnp.exp(s - m)
            denoms.append(jnp.sum(e, axis=-1, keepdims=True) + jnp.exp(sinks[p] - m))
            e = e.astype(BF16)
            zero = jnp.zeros_like(e)
            weights.append(jnp.concatenate([jnp.where(own, zero, e), jnp.where(own, e, zero)], axis=1))
        if jb + 1 < nblk:
            kv = keys_values(jb + 1)
            scores = all_scores(jb + 1, kv[0])
            yield
        pvs = jnp.dot(jnp.concatenate(weights, axis=0), vcat, preferred_element_type=F32)
        for p in range(GROUP_A):
            pv = pvs[p * 2 * BLOCK:(p + 1) * 2 * BLOCK] * (1.0 / denoms[p])
            store(jb, p, jnp.where(lane1 < HEAD_DIM, pv[:BLOCK], pv[BLOCK:]).astype(BF16))
        yield


def _proj_b_kernel(x_ref, mod_ref, g_ref, w_ref, o_ref):
    d = w_ref.shape[0]
    for r in range(0, x_ref.shape[1], PROJ_B_SUB):
        rows = slice(r, r + PROJ_B_SUB)
        h = _rms_mod(x_ref[0, rows, :], g_ref[...], mod_ref[0, 1:2, :],
                     mod_ref[0, 0:1, :]).astype(BF16)
        for n in range(w_ref.shape[1] // d):
            o_ref[0, rows, n * d:(n + 1) * d] = jnp.dot(
                h, w_ref[:, n * d:(n + 1) * d], preferred_element_type=F32).astype(BF16)


def _proj_b(x, mod, gain1, w):
    b, s, d = x.shape
    tm = ROW_TILE
    n = w.shape[1]
    return pl.pallas_call(
        _proj_b_kernel,
        out_shape=jax.ShapeDtypeStruct((b, s, n), BF16),
        grid=(b, s // tm),
        in_specs=[
            pl.BlockSpec((1, tm, d), lambda bi, i: (bi, i, 0)),
            pl.BlockSpec((1, 6, d), lambda bi, i: (bi, 0, 0)),
            _resident((1, d)),
            _resident(w.shape),
        ],
        out_specs=pl.BlockSpec((1, tm, n), lambda bi, i: (bi, i, 0)),
        compiler_params=_params(("parallel", "parallel")),
        name="proj_b",
    )(x, mod, gain1, w)


def _attn_b_kernel(q_ref, k_ref, v_ref, uo_ref, o_ref):
    uo = uo_ref[...]
    nrow = 2 * SB_ROWS
    lane = lax.broadcasted_iota(jnp.int32, (nrow, LANES), 1)
    row = lax.broadcasted_iota(jnp.int32, (nrow, LANES), 0)
    keep = (lane < HEAD_DIM) == (row < SB_ROWS)
    col_minus_row = lane - (row & (SB_ROWS - 1))
    out_head0 = lax.broadcasted_iota(jnp.int32, (SB_ROWS, LANES), 1) < HEAD_DIM
    cutoff = SB_CUTOFF * LOG2E
    sign = jnp.int32(-2 ** 31)

    def stacked_queries(t0):
        q = q_ref[0, pl.ds(t0, SB_ROWS), :]
        qq = jnp.concatenate([q, q], axis=0)
        return jnp.where(keep, qq, jnp.zeros_like(qq))

    def scores(lhs, start, nkeys):
        return lax.dot_general(lhs, k_ref[0, pl.ds(start, nkeys), :], _NT,
                               preferred_element_type=F32)

    def log_pieces(z, mask):
        neg_abs = lax.bitcast_convert_type(lax.bitcast_convert_type(z, jnp.int32) | sign, F32)
        log_beta = jnp.minimum(z, 0.0) - jnp.log2(1.0 + jnp.exp2(neg_abs))
        log_rest = log_beta - z
        if mask is not None:
            log_rest = jnp.where(mask, log_rest, 0.0)
        hi = log_rest.astype(BF16)
        lo = (log_rest - hi.astype(F32)).astype(BF16)
        return log_beta, jnp.concatenate([hi, lo], axis=1)

    def tile_sums(pieces):
        sums = jnp.dot(jnp.concatenate(pieces, axis=0), uo, preferred_element_type=F32)
        return [sums[i * nrow:(i + 1) * nrow] for i in range(len(pieces))]

    def weights(log_beta, sums, later, mask):
        t = log_beta + sums[:, :BLOCK]
        if later is not None:
            t = t + later
        a = jnp.exp2(t)
        if mask is not None:
            a = jnp.where(mask, a, 0.0)
        return a.astype(BF16)

    def store(t0, pv):
        o_ref[0, pl.ds(t0, SB_ROWS), :] = jnp.where(
            out_head0, pv[:SB_ROWS], pv[SB_ROWS:]).astype(BF16)

    def first_windows(units):
        tiles, zs, vals = [], [], []
        for t0, start, limit in units:
            mine = [(b, None if b * BLOCK + BLOCK - 1 < limit else col_minus_row < limit - b * BLOCK)
                    for b in range(SB_KEYS // BLOCK) if b * BLOCK - (SB_ROWS - 1) < limit]
            tiles.append(mine)
            zs.append(scores(stacked_queries(t0), start, len(mine) * BLOCK))
            vals.append(v_ref[0, pl.ds(start, len(mine) * BLOCK), :])
        terms = [dict() for _ in units]
        for b in reversed(range(SB_KEYS // BLOCK)):
            have = [(u, mask) for u, mine in enumerate(tiles) for tb, mask in mine if tb == b]
            parts = [log_pieces(zs[u][:, b * BLOCK:(b + 1) * BLOCK], mask) for u, mask in have]
            for (u, _), (log_beta, _), sums in zip(have, parts, tile_sums([p for _, p in parts])):
                terms[u][b] = (log_beta, sums)
        out = []
        for u, (t0, _, _) in enumerate(units):
            later, ws = None, []
            for b, mask in reversed(tiles[u]):
                log_beta, sums = terms[u][b]
                ws.insert(0, weights(log_beta, sums, later, mask))
                later = sums[:, BLOCK:] if later is None else later + sums[:, BLOCK:]
            pv = jnp.dot(jnp.concatenate(ws, axis=1), vals[u], preferred_element_type=F32)
            store(t0, pv)
            out.append((later, pv))
        return out

    def sweep_earlier(t0, start, total, pv):
        lhs = stacked_queries(t0)

        def cond(st):
            return (st[0] > 0) & (st[1] > 0)

        def body(st):
            end, _, carry, acc = st
            begin = pl.multiple_of(jnp.maximum(end - BLOCK, 0), SB_ROWS)
            mask = lane < end - begin
            log_beta, pieces = log_pieces(scores(lhs, begin, BLOCK), mask)
            sums, = tile_sums([pieces])
            w = weights(log_beta, sums, carry, mask)
            acc = acc + jnp.dot(w, v_ref[0, pl.ds(begin, BLOCK), :], preferred_element_type=F32)
            carry = carry + sums[:, BLOCK:]
            return begin, (jnp.max(carry) > cutoff).astype(jnp.int32), carry, acc

        st = lax.while_loop(cond, body, (jnp.int32(start), jnp.int32(1), total, pv))
        store(t0, st[3])

    def run_units(units):
        res = first_windows(units)
        check = [(u, r) for u, r in zip(units, res) if not (isinstance(u[1], int) and u[1] == 0)]
        if not check:
            return
        flags = [(jnp.max(total) > cutoff).astype(jnp.int32) for _, (total, _) in check]

        @pl.when(functools.reduce(jnp.maximum, flags) > 0)
        def _():
            for ((t0, start, _), (total, pv)), flag in zip(check, flags):
                pl.when(flag > 0)(functools.partial(sweep_earlier, t0, start, total, pv))

    nunit = q_ref.shape[1] // SB_ROWS
    reach = SB_KEYS - SB_ROWS
    nclamp = reach // SB_ROWS
    npeel = nclamp + (nunit - nclamp) % SB_GROUP
    run_units([(u * SB_ROWS, max(u * SB_ROWS - reach, 0), min(u * SB_ROWS, reach))
               for u in range(npeel)])

    def group(g, _):
        base = (npeel + g * SB_GROUP) * SB_ROWS
        t0s = [pl.multiple_of(base + m * SB_ROWS, SB_ROWS) for m in range(SB_GROUP)]
        run_units([(t0, pl.multiple_of(t0 - reach, SB_ROWS), reach) for t0 in t0s])
        return 0

    lax.fori_loop(0, (nunit - npeel) // SB_GROUP, group, 0)


def _attn_b(qkv):
    b, s, n = qkv.shape
    npair = n // 3 // LANES
    assert s % SB_ROWS == 0 and s >= SB_KEYS
    key = jnp.arange(BLOCK)
    later = (key[:, None] > key[None, :]).astype(BF16)
    uo = jnp.concatenate([later, jnp.ones((BLOCK, BLOCK), BF16)], axis=1)
    uo = jnp.concatenate([uo, uo], axis=0)
    blk = lambda off: pl.BlockSpec((1, s, LANES), lambda bi, hp: (bi, 0, off + hp))
    return pl.pallas_call(
        _attn_b_kernel,
        out_shape=jax.ShapeDtypeStruct((b, s, n // 3), BF16),
        grid=(b, npair),
        in_specs=[blk(0), blk(npair), blk(2 * npair), _resident(uo.shape)],
        out_specs=blk(0),
        compiler_params=_params(("parallel", "parallel")),
        name="attn_b",
    )(qkv, qkv, qkv, uo)


def _post_steps(o, x_ref, mod_ref, g_ref, wo_ref, wg_ref, wu_ref, wd_ref, out_ref, acc_ref):
    y = jnp.dot(o, wo_ref[...], preferred_element_type=F32)
    yield
    x1 = x_ref[0] + mod_ref[0, 2:3, :] * y
    h = _rms_mod(x1, g_ref[...], mod_ref[0, 4:5, :], mod_ref[0, 3:4, :]).astype(BF16)
    for f in range(wg_ref.shape[1] // FF_CHUNK):
        cols = slice(f * FF_CHUNK, (f + 1) * FF_CHUNK)
        g = jnp.dot(h, wg_ref[:, cols], preferred_element_type=F32)
        yield
        u = jnp.dot(h, wu_ref[:, cols], preferred_element_type=F32)
        yield
        act = ((g * jax.nn.sigmoid(g)) * u).astype(BF16)
        down = jnp.dot(act, wd_ref[cols, :], preferred_element_type=F32)
        if f == 0:
            acc_ref[...] = down
        else:
            acc_ref[...] += down
        yield
    out_ref[0] = x1 + mod_ref[0, 5:6, :] * acc_ref[...]


def _post_kernel(o_ref, *refs):
    for _ in _post_steps(o_ref[0], *refs):
        pass


def _layer_a_kernel(tiles_per_seq, sink_ref, q_ref, kc_ref, kp_ref, vc_ref, vp_ref, x_ref, mod_ref,
                    g_ref, wo_ref, wg_ref, wu_ref, wd_ref, out_ref, o_ref, acc_ref):
    t = pl.program_id(0)

    @pl.when(t == 0)
    def _():
        o_ref[...] = jnp.zeros_like(o_ref)

    post = _post_steps(o_ref[...], x_ref, mod_ref, g_ref, wo_ref, wg_ref, wu_ref, wd_ref, out_ref,
                       acc_ref)
    next(post)

    def store(jb, p, tile):
        o_ref[jb * BLOCK:(jb + 1) * BLOCK, p * LANES:(p + 1) * LANES] = tile

    last = pl.num_programs(0) - 2
    first = jnp.minimum(t, last) % tiles_per_seq == 0
    attn = _attn_a_steps(first, sink_ref, q_ref, kc_ref, kp_ref, vc_ref, vp_ref, store)
    done = object()
    running = True
    while running:
        running = next(attn, done) is not done
        for _ in range(LAYER_A_RATIO):
            running |= next(post, done) is not done


def _layer_a(sinks, q, k, v, x, wo, mod, gain2, wg, wu, wd):
    b, s, d = x.shape
    nq = q.shape[-1]
    tm = ROW_TILE
    assert tm == ATTN_A_BLOCKS * BLOCK
    n = s // tm
    ntile = b * n
    attn_tile = lambda t: jnp.minimum(t, ntile - 1)
    post_tile = lambda t: jnp.maximum(t - 1, 0)
    cur = lambda t: (attn_tile(t) // n, attn_tile(t) % n, 0)
    prev = lambda t: (attn_tile(t) // n, jnp.maximum((attn_tile(t) % n) * ATTN_A_BLOCKS - 1, 0), 0)
    post = lambda t: (post_tile(t) // n, post_tile(t) % n, 0)
    return pl.pallas_call(
        functools.partial(_layer_a_kernel, n),
        out_shape=jax.ShapeDtypeStruct((b, s, d), F32),
        grid=(ntile + 1,),
        in_specs=[
            pl.BlockSpec(memory_space=pltpu.SMEM),
            pl.BlockSpec((1, tm, nq), cur),
            pl.BlockSpec((1, tm, LANES), cur),
            pl.BlockSpec((1, BLOCK, LANES), prev),
            pl.BlockSpec((1, tm, LANES), cur),
            pl.BlockSpec((1, BLOCK, LANES), prev),
            pl.BlockSpec((1, tm, d), post),
            pl.BlockSpec((1, 6, d), lambda t: (post_tile(t) // n, 0, 0)),
            _resident((1, d)),
            _resident(wo.shape), _resident(wg.shape), _resident(wu.shape), _resident(wd.shape),
        ],
        out_specs=pl.BlockSpec((1, tm, d), post),
        scratch_shapes=[pltpu.VMEM((tm, nq), BF16), pltpu.VMEM((tm, d), F32)],
        compiler_params=_params(("arbitrary",)),
        name="layer_a",
    )(sinks, q, k, k, v, v, x, mod, gain2, wo, wg, wu, wd)


def _post(o, x, wo, mod, gain2, wg, wu, wd):
    b, s, d = x.shape
    tm = ROW_TILE
    row = lambda n: pl.BlockSpec((1, tm, n), lambda bi, i: (bi, i, 0))
    return pl.pallas_call(
        _post_kernel,
        out_shape=jax.ShapeDtypeStruct((b, s, d), F32),
        grid=(b, s // tm),
        in_specs=[
            row(o.shape[-1]), row(d),
            pl.BlockSpec((1, 6, d), lambda bi, i: (bi, 0, 0)),
            _resident((1, d)),
            _resident(wo.shape), _resident(wg.shape), _resident(wu.shape), _resident(wd.shape),
        ],
        out_specs=row(d),
        scratch_shapes=[pltpu.VMEM((tm, d), F32)],
        compiler_params=_params(("parallel", "parallel")),
        name="post",
    )(o, x, mod, gain2, wo, wg, wu, wd)


def _pair_perm():
    cols = []
    for p in range(GROUP_A):
        for h in (p, p + GROUP_A):
            cols.extend(range(h * HEAD_DIM, (h + 1) * HEAD_DIM))
    return jnp.asarray(cols, jnp.int32)


def kernel(x, c, positions, ada_w, ada_b, norm1_g, norm2_g, wqkv_a, q_norm_a, k_norm_a, sinks_a,
           wo_a, wqkv_b, wo_b, w_gate, w_up, w_down):
    b, s, d = x.shape
    depth = ada_w.shape[0]
    assert w_gate.shape[-1] % FF_CHUNK == 0
    scale = HEAD_DIM ** -0.5

    mod = _adaln(c, ada_w, ada_b).reshape(depth, b, 6, d)

    perm = _pair_perm()
    nqa = N_Q_A * HEAD_DIM
    lane = jnp.arange(LANES)
    gsum = (lane[:, None] // HEAD_DIM == lane[None, :] // HEAD_DIM).astype(BF16)

    for i in range(depth):
        j = i // 2
        post_args = (mod[i], norm2_g[i].reshape(1, d), w_gate[i].astype(BF16), w_up[i].astype(BF16),
                     w_down[i].astype(BF16))
        if i % 2 == 0:
            w = wqkv_a[j]
            w = jnp.concatenate([w[:, :nqa][:, perm], w[:, nqa:]], axis=1).astype(BF16)
            gains = jnp.stack([jnp.tile(q_norm_a[j] * scale, 2), jnp.tile(k_norm_a[j], 2)])
            q, k, v = _proj_a(x, mod[i], norm1_g[i].reshape(1, d), w, gains, gsum, positions)
            x = _layer_a(sinks_a[j], q, k, v, x, wo_a[j][perm, :].astype(BF16), *post_args)
        else:
            w = wqkv_b[j]
            nqb = N_H_B * HEAD_DIM
            w = jnp.concatenate([w[:, :nqb] * (scale * LOG2E), w[:, nqb:]], axis=1).astype(BF16)
            o = _attn_b(_proj_b(x, mod[i], norm1_g[i].reshape(1, d), w))
            x = _post(o, x, wo_b[j].astype(BF16), *post_args)
    return x
```

```python
import functools

import jax
import jax.numpy as jnp
from jax import lax
from jax.experimental import pallas as pl
from jax.experimental.pallas import tpu as pltpu

F32 = jnp.float32
BF16 = jnp.bfloat16

HEAD_DIM = 64
N_Q_A = 16
N_KV_A = 2
GROUP_A = N_Q_A // N_KV_A
N_H_B = 16
BLOCK = 128
WINDOW = 128
assert WINDOW == BLOCK
ROT_DIM = HEAD_DIM // 4
ROPE_THETA = 500000.0
EPS = 1e-6
LANES = 128
FF_CHUNK = 256
ROW_TILE = 512
PROJ_A_ROWS = 1024
PROJ_A_SUB = 256
PROJ_B_SUB = 256
ATTN_A_BLOCKS = 4
LAYER_A_RATIO = 6
VMEM_LIMIT = 56 * 1024 * 1024
SB_CUTOFF = -104.0
SB_ROWS = 64
SB_KEYS = 384
SB_GROUP = 19
LOG2E = 1.4426950408889634

_NT = (((1,), (1,)), ((), ()))


def _params(sem):
    return pltpu.CompilerParams(dimension_semantics=sem, vmem_limit_bytes=VMEM_LIMIT)


def _resident(shape):
    return pl.BlockSpec(shape, lambda *_: (0,) * len(shape), pipeline_mode=pl.Buffered(1))


def _rms_mod(x, gain, scale, shift):
    ms = jnp.mean(x * x, axis=-1, keepdims=True)
    y = x * lax.rsqrt(ms + EPS)
    return (y * gain) * (1.0 + scale) + shift


def _split_dot(x, m):
    hi = x.astype(BF16)
    lo = (x - hi.astype(F32)).astype(BF16)
    return (jnp.dot(hi, m, preferred_element_type=F32)
            + jnp.dot(lo, m, preferred_element_type=F32))


def _adaln_kernel(c_ref, w_ref, b_ref, o_ref):
    c = c_ref[...]
    cond = (c * jax.nn.sigmoid(c)).astype(BF16)
    o_ref[0] = jnp.dot(cond, w_ref[0].astype(BF16), preferred_element_type=F32) + b_ref[0]


def _adaln(c, ada_w, ada_b):
    depth, d, n = ada_w.shape
    b = c.shape[0]
    tn = 1024
    return pl.pallas_call(
        _adaln_kernel,
        out_shape=jax.ShapeDtypeStruct((depth, b, n), F32),
        grid=(depth, n // tn),
        in_specs=[
            pl.BlockSpec((b, d), lambda l, j: (0, 0)),
            pl.BlockSpec((1, d, tn), lambda l, j: (l, 0, j)),
            pl.BlockSpec((1, 1, tn), lambda l, j: (l, 0, j)),
        ],
        out_specs=pl.BlockSpec((1, b, tn), lambda l, j: (l, 0, j)),
        compiler_params=_params(("arbitrary", "arbitrary")),
        name="adaln",
    )(c, ada_w, ada_b.reshape(depth, 1, n))


def _rope_consts():
    half = ROT_DIM // 2
    inv_freq = jnp.power(F32(ROPE_THETA), -jnp.arange(half, dtype=F32) * 2.0 / ROT_DIM)
    f = jnp.arange(half)[:, None]
    d = jnp.arange(LANES)[None, :] % HEAD_DIM
    zero = jnp.zeros((half, LANES), F32)
    from_cos = jnp.concatenate([((d == f) | (d == f + half)).astype(F32), zero, zero], axis=1)
    from_sin = jnp.concatenate([zero, -(d == f).astype(F32), (d == f + half).astype(F32)], axis=1)
    place = jnp.concatenate([from_cos] * 3 + [from_sin] * 3, axis=0).astype(BF16)
    base = (d >= ROT_DIM).astype(F32)
    return inv_freq.reshape(half, 1), place, base


def _split3(x):
    hi = x.astype(BF16).astype(F32)
    mid = (x - hi).astype(BF16).astype(F32)
    return [hi, mid, x - hi - mid]


def _proj_a_kernel(x_ref, mod_ref, g_ref, w_ref, gain_ref, gsum_ref, pos_ref, invf_ref, place_ref,
                   base_ref, q_ref, k_ref, v_ref):
    ang = pos_ref[0].astype(F32) * invf_ref[...]
    trig = jnp.concatenate(_split3(jnp.cos(ang)) + _split3(jnp.sin(ang)), axis=0).astype(BF16)
    gsum = gsum_ref[...]
    half = ROT_DIM // 2
    nq = q_ref.shape[-1] // LANES

    def project(rows):
        h = _rms_mod(x_ref[0, rows, :], g_ref[...], mod_ref[0, 1:2, :], mod_ref[0, 0:1, :])
        return jnp.dot(h.astype(BF16), w_ref[...], preferred_element_type=F32)

    def finish(rows, qkv):
        tables = lax.dot_general(trig[:, rows], place_ref[...], (((0,), (0,)), ((), ())),
                                 preferred_element_type=F32)
        cos = tables[:, :LANES] + base_ref[...]
        sup, sdn = tables[:, LANES:2 * LANES], tables[:, 2 * LANES:]

        def norm_rope(blk, gain):
            ssq = _split_dot(blk * blk, gsum)
            y = blk * lax.rsqrt(ssq * (1.0 / HEAD_DIM) + EPS) * gain
            return y * cos + pltpu.roll(y, LANES - half, 1) * sup + pltpu.roll(y, half, 1) * sdn

        for cb in range(nq):
            blk = qkv[:, cb * LANES:(cb + 1) * LANES]
            q_ref[0, rows, cb * LANES:(cb + 1) * LANES] = norm_rope(
                blk, gain_ref[0:1, :]).astype(BF16)
        k_ref[0, rows, :] = norm_rope(qkv[:, nq * LANES:(nq + 1) * LANES],
                                      gain_ref[1:2, :]).astype(BF16)
        v_ref[0, rows, :] = qkv[:, (nq + 1) * LANES:(nq + 2) * LANES].astype(BF16)

    subs = [slice(r, r + PROJ_A_SUB) for r in range(0, x_ref.shape[1], PROJ_A_SUB)]
    pending = None
    for rows in subs:
        qkv = project(rows)
        if pending is not None:
            finish(*pending)
        pending = (rows, qkv)
    finish(*pending)


def _proj_a(x, mod, gain1, w, gains, gsum, positions):
    b, s, d = x.shape
    tm = PROJ_A_ROWS
    nq = N_Q_A * HEAD_DIM
    inv_freq, place, base = _rope_consts()
    row = lambda n: pl.BlockSpec((1, tm, n), lambda bi, i: (bi, i, 0))
    out = lambda n: jax.ShapeDtypeStruct((b, s, n), BF16)
    return pl.pallas_call(
        _proj_a_kernel,
        out_shape=(out(nq), out(LANES), out(LANES)),
        grid=(b, s // tm),
        in_specs=[
            row(d),
            pl.BlockSpec((1, 6, d), lambda bi, i: (bi, 0, 0)),
            _resident((1, d)),
            _resident(w.shape),
            _resident(gains.shape),
            _resident(gsum.shape),
            pl.BlockSpec((1, 1, tm), lambda bi, i: (bi, 0, i)),
            _resident(inv_freq.shape), _resident(place.shape), _resident(base.shape),
        ],
        out_specs=(row(nq), row(LANES), row(LANES)),
        compiler_params=_params(("parallel", "parallel")),
        name="proj_a",
    )(x, mod, gain1, w, gains, gsum, positions.reshape(b, 1, s), inv_freq, place, base)


def _attn_a_steps(first, sink_ref, q_ref, kc_ref, kp_ref, vc_ref, vp_ref, store):
    lane2 = lax.broadcasted_iota(jnp.int32, (2 * BLOCK, LANES), 1)
    row2 = lax.broadcasted_iota(jnp.int32, (2 * BLOCK, LANES), 0)
    keep = (lane2 < HEAD_DIM) == (row2 < BLOCK)
    own = lane2 <= (row2 & (BLOCK - 1))
    rowc = lax.broadcasted_iota(jnp.int32, (2 * BLOCK, 1), 0)
    lane1 = lax.broadcasted_iota(jnp.int32, (BLOCK, LANES), 1)
    sinks = [jnp.where(rowc < BLOCK, sink_ref[p], sink_ref[p + GROUP_A]) for p in range(GROUP_A)]

    def keys_values(jb):
        rows = slice(jb * BLOCK, (jb + 1) * BLOCK)
        before = slice((jb - 1) * BLOCK, jb * BLOCK)
        kprev, vprev = (kp_ref[0], vp_ref[0]) if jb == 0 else (kc_ref[0, before, :], vc_ref[0, before, :])
        return (jnp.concatenate([kprev, kc_ref[0, rows, :]], axis=0),
                jnp.concatenate([vprev, vc_ref[0, rows, :]], axis=0))

    def all_scores(jb, kcat):
        lhs = []
        for p in range(GROUP_A):
            qp = q_ref[0, jb * BLOCK:(jb + 1) * BLOCK, p * LANES:(p + 1) * LANES]
            qq = jnp.concatenate([qp, qp], axis=0)
            lhs.append(jnp.where(keep, qq, jnp.zeros_like(qq)))
        s = lax.dot_general(jnp.concatenate(lhs, axis=0), kcat, _NT, preferred_element_type=F32)
        return [s[p * 2 * BLOCK:(p + 1) * 2 * BLOCK] for p in range(GROUP_A)]

    nblk = q_ref.shape[1] // BLOCK
    kv = keys_values(0)
    scores = all_scores(0, kv[0])
    yield
    for jb in range(nblk):
        vcat = kv[1]
        no_prev = jnp.where(first, -1e30, 0.0).astype(F32) if jb == 0 else None
        weights, denoms = [], []
        for p in range(GROUP_A):
            before = scores[p][:, :BLOCK]
            s = jnp.where(own, scores[p][:, BLOCK:], before if no_prev is None else before + no_prev)
            m = jnp.maximum(jnp.max(s, axis=-1, keepdims=True), sinks[p])
            e = jnp.exp(s - m)
            denoms.append(jnp.sum(e, axis=-1, keepdims=True) + jnp.exp(sinks[p] - m))
            e = e.astype(BF16)
            zero = jnp.zeros_like(e)
            weights.append(jnp.concatenate([jnp.where(own, zero, e), jnp.where(own, e, zero)], axis=1))
        if jb + 1 < nblk:
            kv = keys_values(jb + 1)
            scores = all_scores(jb + 1, kv[0])
            yield
        pvs = jnp.dot(jnp.concatenate(weights, axis=0), vcat, preferred_element_type=F32)
        for p in range(GROUP_A):
            pv = pvs[p * 2 * BLOCK:(p + 1) * 2 * BLOCK] * (1.0 / denoms[p])
            store(jb, p, jnp.where(lane1 < HEAD_DIM, pv[:BLOCK], pv[BLOCK:]).astype(BF16))
        yield


def _proj_b_kernel(x_ref, mod_ref, g_ref, w_ref, o_ref):
    d = w_ref.shape[0]
    for r in range(0, x_ref.shape[1], PROJ_B_SUB):
        rows = slice(r, r + PROJ_B_SUB)
        h = _rms_mod(x_ref[0, rows, :], g_ref[...], mod_ref[0, 1:2, :],
                     mod_ref[0, 0:1, :]).astype(BF16)
        for n in range(w_ref.shape[1] // d):
            o_ref[0, rows, n * d:(n + 1) * d] = jnp.dot(
                h, w_ref[:, n * d:(n + 1) * d], preferred_element_type=F32).astype(BF16)


def _proj_b(x, mod, gain1, w):
    b, s, d = x.shape
    tm = ROW_TILE
    n = w.shape[1]
    return pl.pallas_call(
        _proj_b_kernel,
        out_shape=jax.ShapeDtypeStruct((b, s, n), BF16),
        grid=(b, s // tm),
        in_specs=[
            pl.BlockSpec((1, tm, d), lambda bi, i: (bi, i, 0)),
            pl.BlockSpec((1, 6, d), lambda bi, i: (bi, 0, 0)),
            _resident((1, d)),
            _resident(w.shape),
        ],
        out_specs=pl.BlockSpec((1, tm, n), lambda bi, i: (bi, i, 0)),
        compiler_params=_params(("parallel", "parallel")),
        name="proj_b",
    )(x, mod, gain1, w)


def _attn_b_kernel(q_ref, k_ref, v_ref, uo_ref, o_ref):
    uo = uo_ref[...]
    nrow = 2 * SB_ROWS
    lane = lax.broadcasted_iota(jnp.int32, (nrow, LANES), 1)
    row = lax.broadcasted_iota(jnp.int32, (nrow, LANES), 0)
    keep = (lane < HEAD_DIM) == (row < SB_ROWS)
    col_minus_row = lane - (row & (SB_ROWS - 1))
    out_head0 = lax.broadcasted_iota(jnp.int32, (SB_ROWS, LANES), 1) < HEAD_DIM
    cutoff = SB_CUTOFF * LOG2E
    sign = jnp.int32(-2 ** 31)

    def stacked_queries(t0):
        q = q_ref[0, pl.ds(t0, SB_ROWS), :]
        qq = jnp.concatenate([q, q], axis=0)
        return jnp.where(keep, qq, jnp.zeros_like(qq))

    def scores(lhs, start, nkeys):
        return lax.dot_general(lhs, k_ref[0, pl.ds(start, nkeys), :], _NT,
                               preferred_element_type=F32)

    def log_pieces(z, mask):
        neg_abs = lax.bitcast_convert_type(lax.bitcast_convert_type(z, jnp.int32) | sign, F32)
        log_beta = jnp.minimum(z, 0.0) - jnp.log2(1.0 + jnp.exp2(neg_abs))
        log_rest = log_beta - z
        if mask is not None:
            log_rest = jnp.where(mask, log_rest, 0.0)
        hi = log_rest.astype(BF16)
        lo = (log_rest - hi.astype(F32)).astype(BF16)
        return log_beta, jnp.concatenate([hi, lo], axis=1)

    def tile_sums(pieces):
        sums = jnp.dot(jnp.concatenate(pieces, axis=0), uo, preferred_element_type=F32)
        return [sums[i * nrow:(i + 1) * nrow] for i in range(len(pieces))]

    def weights(log_beta, sums, later, mask):
        t = log_beta + sums[:, :BLOCK]
        if later is not None:
            t = t + later
        a = jnp.exp2(t)
        if mask is not None:
            a = jnp.where(mask, a, 0.0)
        return a.astype(BF16)

    def store(t0, pv):
        o_ref[0, pl.ds(t0, SB_ROWS), :] = jnp.where(
            out_head0, pv[:SB_ROWS], pv[SB_ROWS:]).astype(BF16)

    def first_windows(units):
        tiles, zs, vals = [], [], []
        for t0, start, limit in units:
            mine = [(b, None if b * BLOCK + BLOCK - 1 < limit else col_minus_row < limit - b * BLOCK)
                    for b in range(SB_KEYS // BLOCK) if b * BLOCK - (SB_ROWS - 1) < limit]
            tiles.append(mine)
            zs.append(scores(stacked_queries(t0), start, len(mine) * BLOCK))
            vals.append(v_ref[0, pl.ds(start, len(mine) * BLOCK), :])
        terms = [dict() for _ in units]
        for b in reversed(range(SB_KEYS // BLOCK)):
            have = [(u, mask) for u, mine in enumerate(tiles) for tb, mask in mine if tb == b]
            parts = [log_pieces(zs[u][:, b * BLOCK:(b + 1) * BLOCK], mask) for u, mask in have]
            for (u, _), (log_beta, _), sums in zip(have, parts, tile_sums([p for _, p in parts])):
                terms[u][b] = (log_beta, sums)
        out = []
        for u, (t0, _, _) in enumerate(units):
            later, ws = None, []
            for b, mask in reversed(tiles[u]):
                log_beta, sums = terms[u][b]
                ws.insert(0, weights(log_beta, sums, later, mask))
                later = sums[:, BLOCK:] if later is None else later + sums[:, BLOCK:]
            pv = jnp.dot(jnp.concatenate(ws, axis=1), vals[u], preferred_element_type=F32)
            store(t0, pv)
            out.append((later, pv))
        return out

    def sweep_earlier(t0, start, total, pv):
        lhs = stacked_queries(t0)

        def cond(st):
            return (st[0] > 0) & (st[1] > 0)

        def body(st):
            end, _, carry, acc = st
            begin = pl.multiple_of(jnp.maximum(end - BLOCK, 0), SB_ROWS)
            mask = lane < end - begin
            log_beta, pieces = log_pieces(scores(lhs, begin, BLOCK), mask)
            sums, = tile_sums([pieces])
            w = weights(log_beta, sums, carry, mask)
            acc = acc + jnp.dot(w, v_ref[0, pl.ds(begin, BLOCK), :], preferred_element_type=F32)
            carry = carry + sums[:, BLOCK:]
            return begin, (jnp.max(carry) > cutoff).astype(jnp.int32), carry, acc

        st = lax.while_loop(cond, body, (jnp.int32(start), jnp.int32(1), total, pv))
        store(t0, st[3])

    def run_units(units):
        res = first_windows(units)
        check = [(u, r) for u, r in zip(units, res) if not (isinstance(u[1], int) and u[1] == 0)]
        if not check:
            return
        flags = [(jnp.max(total) > cutoff).astype(jnp.int32) for _, (total, _) in check]

        @pl.when(functools.reduce(jnp.maximum, flags) > 0)
        def _():
            for ((t0, start, _), (total, pv)), flag in zip(check, flags):
                pl.when(flag > 0)(functools.partial(sweep_earlier, t0, start, total, pv))

    nunit = q_ref.shape[1] // SB_ROWS
    reach = SB_KEYS - SB_ROWS
    nclamp = reach // SB_ROWS
    npeel = nclamp + (nunit - nclamp) % SB_GROUP
    run_units([(u * SB_ROWS, max(u * SB_ROWS - reach, 0), min(u * SB_ROWS, reach))
               for u in range(npeel)])

    def group(g, _):
        base = (npeel + g * SB_GROUP) * SB_ROWS
        t0s = [pl.multiple_of(base + m * SB_ROWS, SB_ROWS) for m in range(SB_GROUP)]
        run_units([(t0, pl.multiple_of(t0 - reach, SB_ROWS), reach) for t0 in t0s])
        return 0

    lax.fori_loop(0, (nunit - npeel) // SB_GROUP, group, 0)


def _attn_b(qkv):
    b, s, n = qkv.shape
    npair = n // 3 // LANES
    assert s % SB_ROWS == 0 and s >= SB_KEYS
    key = jnp.arange(BLOCK)
    later = (key[:, None] > key[None, :]).astype(BF16)
    uo = jnp.concatenate([later, jnp.ones((BLOCK, BLOCK), BF16)], axis=1)
    uo = jnp.concatenate([uo, uo], axis=0)
    blk = lambda off: pl.BlockSpec((1, s, LANES), lambda bi, hp: (bi, 0, off + hp))
    return pl.pallas_call(
        _attn_b_kernel,
        out_shape=jax.ShapeDtypeStruct((b, s, n // 3), BF16),
        grid=(b, npair),
        in_specs=[blk(0), blk(npair), blk(2 * npair), _resident(uo.shape)],
        out_specs=blk(0),
        compiler_params=_params(("parallel", "parallel")),
        name="attn_b",
    )(qkv, qkv, qkv, uo)


def _post_steps(o, x_ref, mod_ref, g_ref, wo_ref, wg_ref, wu_ref, wd_ref, out_ref, acc_ref):
    y = jnp.dot(o, wo_ref[...], preferred_element_type=F32)
    yield
    x1 = x_ref[0] + mod_ref[0, 2:3, :] * y
    h = _rms_mod(x1, g_ref[...], mod_ref[0, 4:5, :], mod_ref[0, 3:4, :]).astype(BF16)
    for f in range(wg_ref.shape[1] // FF_CHUNK):
        cols = slice(f * FF_CHUNK, (f + 1) * FF_CHUNK)
        g = jnp.dot(h, wg_ref[:, cols], preferred_element_type=F32)
        yield
        u = jnp.dot(h, wu_ref[:, cols], preferred_element_type=F32)
        yield
        act = ((g * jax.nn.sigmoid(g)) * u).astype(BF16)
        down = jnp.dot(act, wd_ref[cols, :], preferred_element_type=F32)
        if f == 0:
            acc_ref[...] = down
        else:
            acc_ref[...] += down
        yield
    out_ref[0] = x1 + mod_ref[0, 5:6, :] * acc_ref[...]


def _post_kernel(o_ref, *refs):
    for _ in _post_steps(o_ref[0], *refs):
        pass


def _layer_a_kernel(tiles_per_seq, sink_ref, q_ref, kc_ref, kp_ref, vc_ref, vp_ref, x_ref, mod_ref,
                    g_ref, wo_ref, wg_ref, wu_ref, wd_ref, out_ref, o_ref, acc_ref):
    t = pl.program_id(0)

    @pl.when(t == 0)
    def _():
        o_ref[...] = jnp.zeros_like(o_ref)

    post = _post_steps(o_ref[...], x_ref, mod_ref, g_ref, wo_ref, wg_ref, wu_ref, wd_ref, out_ref,
                       acc_ref)
    next(post)

    def store(jb, p, tile):
        o_ref[jb * BLOCK:(jb + 1) * BLOCK, p * LANES:(p + 1) * LANES] = tile

    last = pl.num_programs(0) - 2
    first = jnp.minimum(t, last) % tiles_per_seq == 0
    attn = _attn_a_steps(first, sink_ref, q_ref, kc_ref, kp_ref, vc_ref, vp_ref, store)
    done = object()
    running = True
    while running:
        running = next(attn, done) is not done
        for _ in range(LAYER_A_RATIO):
            running |= next(post, done) is not done


def _layer_a(sinks, q, k, v, x, wo, mod, gain2, wg, wu, wd):
    b, s, d = x.shape
    nq = q.shape[-1]
    tm = ROW_TILE
    assert tm == ATTN_A_BLOCKS * BLOCK
    n = s // tm
    ntile = b * n
    attn_tile = lambda t: jnp.minimum(t, ntile - 1)
    post_tile = lambda t: jnp.maximum(t - 1, 0)
    cur = lambda t: (attn_tile(t) // n, attn_tile(t) % n, 0)
    prev = lambda t: (attn_tile(t) // n, jnp.maximum((attn_tile(t) % n) * ATTN_A_BLOCKS - 1, 0), 0)
    post = lambda t: (post_tile(t) // n, post_tile(t) % n, 0)
    return pl.pallas_call(
        functools.partial(_layer_a_kernel, n),
        out_shape=jax.ShapeDtypeStruct((b, s, d), F32),
        grid=(ntile + 1,),
        in_specs=[
            pl.BlockSpec(memory_space=pltpu.SMEM),
            pl.BlockSpec((1, tm, nq), cur),
            pl.BlockSpec((1, tm, LANES), cur),
            pl.BlockSpec((1, BLOCK, LANES), prev),
            pl.BlockSpec((1, tm, LANES), cur),
            pl.BlockSpec((1, BLOCK, LANES), prev),
            pl.BlockSpec((1, tm, d), post),
            pl.BlockSpec((1, 6, d), lambda t: (post_tile(t) // n, 0, 0)),
            _resident((1, d)),
            _resident(wo.shape), _resident(wg.shape), _resident(wu.shape), _resident(wd.shape),
        ],
        out_specs=pl.BlockSpec((1, tm, d), post),
        scratch_shapes=[pltpu.VMEM((tm, nq), BF16), pltpu.VMEM((tm, d), F32)],
        compiler_params=_params(("arbitrary",)),
        name="layer_a",
    )(sinks, q, k, k, v, v, x, mod, gain2, wo, wg, wu, wd)


def _post(o, x, wo, mod, gain2, wg, wu, wd):
    b, s, d = x.shape
    tm = ROW_TILE
    row = lambda n: pl.BlockSpec((1, tm, n), lambda bi, i: (bi, i, 0))
    return pl.pallas_call(
        _post_kernel,
        out_shape=jax.ShapeDtypeStruct((b, s, d), F32),
        grid=(b, s // tm),
        in_specs=[
            row(o.shape[-1]), row(d),
            pl.BlockSpec((1, 6, d), lambda bi, i: (bi, 0, 0)),
            _resident((1, d)),
            _resident(wo.shape), _resident(wg.shape), _resident(wu.shape), _resident(wd.shape),
        ],
        out_specs=row(d),
        scratch_shapes=[pltpu.VMEM((tm, d), F32)],
        compiler_params=_params(("parallel", "parallel")),
        name="post",
    )(o, x, mod, gain2, wo, wg, wu, wd)


def _pair_perm():
    cols = []
    for p in range(GROUP_A):
        for h in (p, p + GROUP_A):
            cols.extend(range(h * HEAD_DIM, (h + 1) * HEAD_DIM))
    return jnp.asarray(cols, jnp.int32)


def kernel(x, c, positions, ada_w, ada_b, norm1_g, norm2_g, wqkv_a, q_norm_a, k_norm_a, sinks_a,
           wo_a, wqkv_b, wo_b, w_gate, w_up, w_down):
    b, s, d = x.shape
    depth = ada_w.shape[0]
    assert w_gate.shape[-1] % FF_CHUNK == 0
    scale = HEAD_DIM ** -0.5

    mod = _adaln(c, ada_w, ada_b).reshape(depth, b, 6, d)

    perm = _pair_perm()
    nqa = N_Q_A * HEAD_DIM
    lane = jnp.arange(LANES)
    gsum = (lane[:, None] // HEAD_DIM == lane[None, :] // HEAD_DIM).astype(BF16)

    for i in range(depth):
        j = i // 2
        post_args = (mod[i], norm2_g[i].reshape(1, d), w_gate[i].astype(BF16), w_up[i].astype(BF16),
                     w_down[i].astype(BF16))
        if i % 2 == 0:
            w = wqkv_a[j]
            w = jnp.concatenate([w[:, :nqa][:, perm], w[:, nqa:]], axis=1).astype(BF16)
            gains = jnp.stack([jnp.tile(q_norm_a[j] * scale, 2), jnp.tile(k_norm_a[j], 2)])
            q, k, v = _proj_a(x, mod[i], norm1_g[i].reshape(1, d), w, gains, gsum, positions)
            x = _layer_a(sinks_a[j], q, k, v, x, wo_a[j][perm, :].astype(BF16), *post_args)
        else:
            w = wqkv_b[j]
            nqb = N_H_B * HEAD_DIM
            w = jnp.concatenate([w[:, :nqb] * (scale * LOG2E), w[:, nqb:]], axis=1).astype(BF16)
            o = _attn_b(_proj_b(x, mod[i], norm1_g[i].reshape(1, d), w))
            x = _post(o, x, wo_b[j].astype(BF16), *post_args)
    return x
```

```python
import functools

import jax
import jax.numpy as jnp
from jax import lax
from jax.experimental import pallas as pl
from jax.experimental.pallas import tpu as pltpu

F32 = jnp.float32
BF16 = jnp.bfloat16

HEAD_DIM = 64
N_Q_A = 16
N_KV_A = 2
GROUP_A = N_Q_A // N_KV_A
N_H_B = 16
BLOCK = 128
WINDOW = 128
assert WINDOW == BLOCK
ROT_DIM = HEAD_DIM // 4
ROPE_THETA = 500000.0
EPS = 1e-6
LANES = 128
FF_CHUNK = 256
ROW_TILE = 512
PROJ_A_ROWS = 1024
PROJ_A_SUB = 256
PROJ_B_SUB = 256
ATTN_A_BLOCKS = 4
LAYER_A_RATIO = 6
VMEM_LIMIT = 56 * 1024 * 1024
SB_CUTOFF = -104.0
SB_ROWS = 64
SB_KEYS = 384
SB_GROUP = 29
LOG2E = 1.4426950408889634

_NT = (((1,), (1,)), ((), ()))


def _params(sem):
    return pltpu.CompilerParams(dimension_semantics=sem, vmem_limit_bytes=VMEM_LIMIT)


def _resident(shape):
    return pl.BlockSpec(shape, lambda *_: (0,) * len(shape), pipeline_mode=pl.Buffered(1))


def _rms_mod(x, gain, scale, shift):
    ms = jnp.mean(x * x, axis=-1, keepdims=True)
    y = x * lax.rsqrt(ms + EPS)
    return (y * gain) * (1.0 + scale) + shift


def _split_dot(x, m):
    hi = x.astype(BF16)
    lo = (x - hi.astype(F32)).astype(BF16)
    return (jnp.dot(hi, m, preferred_element_type=F32)
            + jnp.dot(lo, m, preferred_element_type=F32))


def _adaln_kernel(c_ref, w_ref, b_ref, o_ref):
    c = c_ref[...]
    cond = (c * jax.nn.sigmoid(c)).astype(BF16)
    o_ref[0] = jnp.dot(cond, w_ref[0].astype(BF16), preferred_element_type=F32) + b_ref[0]


def _adaln(c, ada_w, ada_b):
    depth, d, n = ada_w.shape
    b = c.shape[0]
    tn = 1024
    return pl.pallas_call(
        _adaln_kernel,
        out_shape=jax.ShapeDtypeStruct((depth, b, n), F32),
        grid=(depth, n // tn),
        in_specs=[
            pl.BlockSpec((b, d), lambda l, j: (0, 0)),
            pl.BlockSpec((1, d, tn), lambda l, j: (l, 0, j)),
            pl.BlockSpec((1, 1, tn), lambda l, j: (l, 0, j)),
        ],
        out_specs=pl.BlockSpec((1, b, tn), lambda l, j: (l, 0, j)),
        compiler_params=_params(("arbitrary", "arbitrary")),
        name="adaln",
    )(c, ada_w, ada_b.reshape(depth, 1, n))


def _rope_consts():
    half = ROT_DIM // 2
    inv_freq = jnp.power(F32(ROPE_THETA), -jnp.arange(half, dtype=F32) * 2.0 / ROT_DIM)
    f = jnp.arange(half)[:, None]
    d = jnp.arange(LANES)[None, :] % HEAD_DIM
    zero = jnp.zeros((half, LANES), F32)
    from_cos = jnp.concatenate([((d == f) | (d == f + half)).astype(F32), zero, zero], axis=1)
    from_sin = jnp.concatenate([zero, -(d == f).astype(F32), (d == f + half).astype(F32)], axis=1)
    place = jnp.concatenate([from_cos] * 3 + [from_sin] * 3, axis=0).astype(BF16)
    base = (d >= ROT_DIM).astype(F32)
    return inv_freq.reshape(half, 1), place, base


def _split3(x):
    hi = x.astype(BF16).astype(F32)
    mid = (x - hi).astype(BF16).astype(F32)
    return [hi, mid, x - hi - mid]


def _proj_a_kernel(x_ref, mod_ref, g_ref, w_ref, gain_ref, gsum_ref, pos_ref, invf_ref, place_ref,
                   base_ref, q_ref, k_ref, v_ref):
    ang = pos_ref[0].astype(F32) * invf_ref[...]
    trig = jnp.concatenate(_split3(jnp.cos(ang)) + _split3(jnp.sin(ang)), axis=0).astype(BF16)
    gsum = gsum_ref[...]
    half = ROT_DIM // 2
    nq = q_ref.shape[-1] // LANES

    def project(rows):
        h = _rms_mod(x_ref[0, rows, :], g_ref[...], mod_ref[0, 1:2, :], mod_ref[0, 0:1, :])
        return jnp.dot(h.astype(BF16), w_ref[...], preferred_element_type=F32)

    def finish(rows, qkv):
        tables = lax.dot_general(trig[:, rows], place_ref[...], (((0,), (0,)), ((), ())),
                                 preferred_element_type=F32)
        cos = tables[:, :LANES] + base_ref[...]
        sup, sdn = tables[:, LANES:2 * LANES], tables[:, 2 * LANES:]

        def norm_rope(blk, gain):
            ssq = _split_dot(blk * blk, gsum)
            y = blk * lax.rsqrt(ssq * (1.0 / HEAD_DIM) + EPS) * gain
            return y * cos + pltpu.roll(y, LANES - half, 1) * sup + pltpu.roll(y, half, 1) * sdn

        for cb in range(nq):
            blk = qkv[:, cb * LANES:(cb + 1) * LANES]
            q_ref[0, rows, cb * LANES:(cb + 1) * LANES] = norm_rope(
                blk, gain_ref[0:1, :]).astype(BF16)
        k_ref[0, rows, :] = norm_rope(qkv[:, nq * LANES:(nq + 1) * LANES],
                                      gain_ref[1:2, :]).astype(BF16)
        v_ref[0, rows, :] = qkv[:, (nq + 1) * LANES:(nq + 2) * LANES].astype(BF16)

    subs = [slice(r, r + PROJ_A_SUB) for r in range(0, x_ref.shape[1], PROJ_A_SUB)]
    pending = None
    for rows in subs:
        qkv = project(rows)
        if pending is not None:
            finish(*pending)
        pending = (rows, qkv)
    finish(*pending)


def _proj_a(x, mod, gain1, w, gains, gsum, positions):
    b, s, d = x.shape
    tm = PROJ_A_ROWS
    nq = N_Q_A * HEAD_DIM
    inv_freq, place, base = _rope_consts()
    row = lambda n: pl.BlockSpec((1, tm, n), lambda bi, i: (bi, i, 0))
    out = lambda n: jax.ShapeDtypeStruct((b, s, n), BF16)
    return pl.pallas_call(
        _proj_a_kernel,
        out_shape=(out(nq), out(LANES), out(LANES)),
        grid=(b, s // tm),
        in_specs=[
            row(d),
            pl.BlockSpec((1, 6, d), lambda bi, i: (bi, 0, 0)),
            _resident((1, d)),
            _resident(w.shape),
            _resident(gains.shape),
            _resident(gsum.shape),
            pl.BlockSpec((1, 1, tm), lambda bi, i: (bi, 0, i)),
            _resident(inv_freq.shape), _resident(place.shape), _resident(base.shape),
        ],
        out_specs=(row(nq), row(LANES), row(LANES)),
        compiler_params=_params(("parallel", "parallel")),
        name="proj_a",
    )(x, mod, gain1, w, gains, gsum, positions.reshape(b, 1, s), inv_freq, place, base)


def _attn_a_steps(first, sink_ref, q_ref, kc_ref, kp_ref, vc_ref, vp_ref, store):
    lane2 = lax.broadcasted_iota(jnp.int32, (2 * BLOCK, LANES), 1)
    row2 = lax.broadcasted_iota(jnp.int32, (2 * BLOCK, LANES), 0)
    keep = (lane2 < HEAD_DIM) == (row2 < BLOCK)
    own = lane2 <= (row2 & (BLOCK - 1))
    rowc = lax.broadcasted_iota(jnp.int32, (2 * BLOCK, 1), 0)
    lane1 = lax.broadcasted_iota(jnp.int32, (BLOCK, LANES), 1)
    sinks = [jnp.where(rowc < BLOCK, sink_ref[p], sink_ref[p + GROUP_A]) for p in range(GROUP_A)]

    def keys_values(jb):
        rows = slice(jb * BLOCK, (jb + 1) * BLOCK)
        before = slice((jb - 1) * BLOCK, jb * BLOCK)
        kprev, vprev = (kp_ref[0], vp_ref[0]) if jb == 0 else (kc_ref[0, before, :], vc_ref[0, before, :])
        return (jnp.concatenate([kprev, kc_ref[0, rows, :]], axis=0),
                jnp.concatenate([vprev, vc_ref[0, rows, :]], axis=0))

    def all_scores(jb, kcat):
        lhs = []
        for p in range(GROUP_A):
            qp = q_ref[0, jb * BLOCK:(jb + 1) * BLOCK, p * LANES:(p + 1) * LANES]
            qq = jnp.concatenate([qp, qp], axis=0)
            lhs.append(jnp.where(keep, qq, jnp.zeros_like(qq)))
        s = lax.dot_general(jnp.concatenate(lhs, axis=0), kcat, _NT, preferred_element_type=F32)
        return [s[p * 2 * BLOCK:(p + 1) * 2 * BLOCK] for p in range(GROUP_A)]

    nblk = q_ref.shape[1] // BLOCK
    kv = keys_values(0)
    scores = all_scores(0, kv[0])
    yield
    for jb in range(nblk):
        vcat = kv[1]
        no_prev = jnp.where(first, -1e30, 0.0).astype(F32) if jb == 0 else None
        weights, denoms = [], []
        for p in range(GROUP_A):
            before = scores[p][:, :BLOCK]
            s = jnp.where(own, scores[p][:, BLOCK:], before if no_prev is None else before + no_prev)
            m = jnp.maximum(jnp.max(s, axis=-1, keepdims=True), sinks[p])
            e = jnp.exp(s - m)
            denoms.append(jnp.sum(e, axis=-1, keepdims=True) + jnp.exp(sinks[p] - m))
            e = e.astype(BF16)
            zero = jnp.zeros_like(e)
            weights.append(jnp.concatenate([jnp.where(own, zero, e), jnp.where(own, e, zero)], axis=1))
        if jb + 1 < nblk:
            kv = keys_values(jb + 1)
            scores = all_scores(jb + 1, kv[0])
            yield
        pvs = jnp.dot(jnp.concatenate(weights, axis=0), vcat, preferred_element_type=F32)
        for p in range(GROUP_A):
            pv = pvs[p * 2 * BLOCK:(p + 1) * 2 * BLOCK] * (1.0 / denoms[p])
            store(jb, p, jnp.where(lane1 < HEAD_DIM, pv[:BLOCK], pv[BLOCK:]).astype(BF16))
        yield


def _proj_b_kernel(x_ref, mod_ref, g_ref, w_ref, o_ref):
    d = w_ref.shape[0]
    for r in range(0, x_ref.shape[1], PROJ_B_SUB):
        rows = slice(r, r + PROJ_B_SUB)
        h = _rms_mod(x_ref[0, rows, :], g_ref[...], mod_ref[0, 1:2, :],
                     mod_ref[0, 0:1, :]).astype(BF16)
        for n in range(w_ref.shape[1] // d):
            o_ref[0, rows, n * d:(n + 1) * d] = jnp.dot(
                h, w_ref[:, n * d:(n + 1) * d], preferred_element_type=F32).astype(BF16)


def _proj_b(x, mod, gain1, w):
    b, s, d = x.shape
    tm = ROW_TILE
    n = w.shape[1]
    return pl.pallas_call(
        _proj_b_kernel,
        out_shape=jax.ShapeDtypeStruct((b, s, n), BF16),
        grid=(b, s // tm),
        in_specs=[
            pl.BlockSpec((1, tm, d), lambda bi, i: (bi, i, 0)),
            pl.BlockSpec((1, 6, d), lambda bi, i: (bi, 0, 0)),
            _resident((1, d)),
            _resident(w.shape),
        ],
        out_specs=pl.BlockSpec((1, tm, n), lambda bi, i: (bi, i, 0)),
        compiler_params=_params(("parallel", "parallel")),
        name="proj_b",
    )(x, mod, gain1, w)


def _attn_b_kernel(q_ref, k_ref, v_ref, uo_ref, o_ref):
    uo = uo_ref[...]
    nrow = 2 * SB_ROWS
    lane = lax.broadcasted_iota(jnp.int32, (nrow, LANES), 1)
    row = lax.broadcasted_iota(jnp.int32, (nrow, LANES), 0)
    keep = (lane < HEAD_DIM) == (row < SB_ROWS)
    col_minus_row = lane - (row & (SB_ROWS - 1))
    out_head0 = lax.broadcasted_iota(jnp.int32, (SB_ROWS, LANES), 1) < HEAD_DIM
    cutoff = SB_CUTOFF * LOG2E
    sign = jnp.int32(-2 ** 31)

    def stacked_queries(t0):
        q = q_ref[0, pl.ds(t0, SB_ROWS), :]
        qq = jnp.concatenate([q, q], axis=0)
        return jnp.where(keep, qq, jnp.zeros_like(qq))

    def scores(lhs, start, nkeys):
        return lax.dot_general(lhs, k_ref[0, pl.ds(start, nkeys), :], _NT,
                               preferred_element_type=F32)

    def log_pieces(z, mask):
        neg_abs = lax.bitcast_convert_type(lax.bitcast_convert_type(z, jnp.int32) | sign, F32)
        log_beta = jnp.minimum(z, 0.0) - jnp.log2(1.0 + jnp.exp2(neg_abs))
        log_rest = log_beta - z
        if mask is not None:
            log_rest = jnp.where(mask, log_rest, 0.0)
        hi = log_rest.astype(BF16)
        lo = (log_rest - hi.astype(F32)).astype(BF16)
        return log_beta, jnp.concatenate([hi, lo], axis=1)

    def tile_sums(pieces):
        sums = jnp.dot(jnp.concatenate(pieces, axis=0), uo, preferred_element_type=F32)
        return [sums[i * nrow:(i + 1) * nrow] for i in range(len(pieces))]

    def weights(log_beta, sums, later, mask):
        t = log_beta + sums[:, :BLOCK]
        if later is not None:
            t = t + later
        a = jnp.exp2(t)
        if mask is not None:
            a = jnp.where(mask, a, 0.0)
        return a.astype(BF16)

    def store(t0, pv):
        o_ref[0, pl.ds(t0, SB_ROWS), :] = jnp.where(
            out_head0, pv[:SB_ROWS], pv[SB_ROWS:]).astype(BF16)

    def first_windows(units):
        tiles, zs, vals = [], [], []
        for t0, start, limit in units:
            mine = [(b, None if b * BLOCK + BLOCK - 1 < limit else col_minus_row < limit - b * BLOCK)
                    for b in range(SB_KEYS // BLOCK) if b * BLOCK - (SB_ROWS - 1) < limit]
            tiles.append(mine)
            zs.append(scores(stacked_queries(t0), start, len(mine) * BLOCK))
            vals.append(v_ref[0, pl.ds(start, len(mine) * BLOCK), :])
        terms = [dict() for _ in units]
        for b in reversed(range(SB_KEYS // BLOCK)):
            have = [(u, mask) for u, mine in enumerate(tiles) for tb, mask in mine if tb == b]
            parts = [log_pieces(zs[u][:, b * BLOCK:(b + 1) * BLOCK], mask) for u, mask in have]
            for (u, _), (log_beta, _), sums in zip(have, parts, tile_sums([p for _, p in parts])):
                terms[u][b] = (log_beta, sums)
        out = []
        for u, (t0, _, _) in enumerate(units):
            later, ws = None, []
            for b, mask in reversed(tiles[u]):
                log_beta, sums = terms[u][b]
                ws.insert(0, weights(log_beta, sums, later, mask))
                later = sums[:, BLOCK:] if later is None else later + sums[:, BLOCK:]
            pv = jnp.dot(jnp.concatenate(ws, axis=1), vals[u], preferred_element_type=F32)
            store(t0, pv)
            out.append((later, pv))
        return out

    def sweep_earlier(t0, start, total, pv):
        lhs = stacked_queries(t0)

        def cond(st):
            return (st[0] > 0) & (st[1] > 0)

        def body(st):
            end, _, carry, acc = st
            begin = pl.multiple_of(jnp.maximum(end - BLOCK, 0), SB_ROWS)
            mask = lane < end - begin
            log_beta, pieces = log_pieces(scores(lhs, begin, BLOCK), mask)
            sums, = tile_sums([pieces])
            w = weights(log_beta, sums, carry, mask)
            acc = acc + jnp.dot(w, v_ref[0, pl.ds(begin, BLOCK), :], preferred_element_type=F32)
            carry = carry + sums[:, BLOCK:]
            return begin, (jnp.max(carry) > cutoff).astype(jnp.int32), carry, acc

        st = lax.while_loop(cond, body, (jnp.int32(start), jnp.int32(1), total, pv))
        store(t0, st[3])

    def run_units(units):
        res = first_windows(units)
        check = [(u, r) for u, r in zip(units, res) if not (isinstance(u[1], int) and u[1] == 0)]
        if not check:
            return
        flags = [(jnp.max(total) > cutoff).astype(jnp.int32) for _, (total, _) in check]

        @pl.when(functools.reduce(jnp.maximum, flags) > 0)
        def _():
            for ((t0, start, _), (total, pv)), flag in zip(check, flags):
                pl.when(flag > 0)(functools.partial(sweep_earlier, t0, start, total, pv))

    nunit = q_ref.shape[1] // SB_ROWS
    reach = SB_KEYS - SB_ROWS
    nclamp = reach // SB_ROWS
    npeel = nclamp + (nunit - nclamp) % SB_GROUP
    run_units([(u * SB_ROWS, max(u * SB_ROWS - reach, 0), min(u * SB_ROWS, reach))
               for u in range(npeel)])

    def group(g, _):
        base = (npeel + g * SB_GROUP) * SB_ROWS
        t0s = [pl.multiple_of(base + m * SB_ROWS, SB_ROWS) for m in range(SB_GROUP)]
        run_units([(t0, pl.multiple_of(t0 - reach, SB_ROWS), reach) for t0 in t0s])
        return 0

    lax.fori_loop(0, (nunit - npeel) // SB_GROUP, group, 0)


def _attn_b(qkv):
    b, s, n = qkv.shape
    npair = n // 3 // LANES
    assert s % SB_ROWS == 0 and s >= SB_KEYS
    key = jnp.arange(BLOCK)
    later = (key[:, None] > key[None, :]).astype(BF16)
    uo = jnp.concatenate([later, jnp.ones((BLOCK, BLOCK), BF16)], axis=1)
    uo = jnp.concatenate([uo, uo], axis=0)
    blk = lambda off: pl.BlockSpec((1, s, LANES), lambda bi, hp: (bi, 0, off + hp))
    return pl.pallas_call(
        _attn_b_kernel,
        out_shape=jax.ShapeDtypeStruct((b, s, n // 3), BF16),
        grid=(b, npair),
        in_specs=[blk(0), blk(npair), blk(2 * npair), _resident(uo.shape)],
        out_specs=blk(0),
        compiler_params=_params(("parallel", "parallel")),
        name="attn_b",
    )(qkv, qkv, qkv, uo)


def _post_steps(o, x_ref, mod_ref, g_ref, wo_ref, wg_ref, wu_ref, wd_ref, out_ref, acc_ref):
    y = jnp.dot(o, wo_ref[...], preferred_element_type=F32)
    yield
    x1 = x_ref[0] + mod_ref[0, 2:3, :] * y
    h = _rms_mod(x1, g_ref[...], mod_ref[0, 4:5, :], mod_ref[0, 3:4, :]).astype(BF16)
    for f in range(wg_ref.shape[1] // FF_CHUNK):
        cols = slice(f * FF_CHUNK, (f + 1) * FF_CHUNK)
        g = jnp.dot(h, wg_ref[:, cols], preferred_element_type=F32)
        yield
        u = jnp.dot(h, wu_ref[:, cols], preferred_element_type=F32)
        yield
        act = ((g * jax.nn.sigmoid(g)) * u).astype(BF16)
        down = jnp.dot(act, wd_ref[cols, :], preferred_element_type=F32)
        if f == 0:
            acc_ref[...] = down
        else:
            acc_ref[...] += down
        yield
    out_ref[0] = x1 + mod_ref[0, 5:6, :] * acc_ref[...]


def _post_kernel(o_ref, *refs):
    for _ in _post_steps(o_ref[0], *refs):
        pass


def _layer_a_kernel(tiles_per_seq, sink_ref, q_ref, kc_ref, kp_ref, vc_ref, vp_ref, x_ref, mod_ref,
                    g_ref, wo_ref, wg_ref, wu_ref, wd_ref, out_ref, o_ref, acc_ref):
    t = pl.program_id(0)

    @pl.when(t == 0)
    def _():
        o_ref[...] = jnp.zeros_like(o_ref)

    post = _post_steps(o_ref[...], x_ref, mod_ref, g_ref, wo_ref, wg_ref, wu_ref, wd_ref, out_ref,
                       acc_ref)
    next(post)

    def store(jb, p, tile):
        o_ref[jb * BLOCK:(jb + 1) * BLOCK, p * LANES:(p + 1) * LANES] = tile

    last = pl.num_programs(0) - 2
    first = jnp.minimum(t, last) % tiles_per_seq == 0
    attn = _attn_a_steps(first, sink_ref, q_ref, kc_ref, kp_ref, vc_ref, vp_ref, store)
    done = object()
    running = True
    while running:
        running = next(attn, done) is not done
        for _ in range(LAYER_A_RATIO):
            running |= next(post, done) is not done


def _layer_a(sinks, q, k, v, x, wo, mod, gain2, wg, wu, wd):
    b, s, d = x.shape
    nq = q.shape[-1]
    tm = ROW_TILE
    assert tm == ATTN_A_BLOCKS * BLOCK
    n = s // tm
    ntile = b * n
    attn_tile = lambda t: jnp.minimum(t, ntile - 1)
    post_tile = lambda t: jnp.maximum(t - 1, 0)
    cur = lambda t: (attn_tile(t) // n, attn_tile(t) % n, 0)
    prev = lambda t: (attn_tile(t) // n, jnp.maximum((attn_tile(t) % n) * ATTN_A_BLOCKS - 1, 0), 0)
    post = lambda t: (post_tile(t) // n, post_tile(t) % n, 0)
    return pl.pallas_call(
        functools.partial(_layer_a_kernel, n),
        out_shape=jax.ShapeDtypeStruct((b, s, d), F32),
        grid=(ntile + 1,),
        in_specs=[
            pl.BlockSpec(memory_space=pltpu.SMEM),
            pl.BlockSpec((1, tm, nq), cur),
            pl.BlockSpec((1, tm, LANES), cur),
            pl.BlockSpec((1, BLOCK, LANES), prev),
            pl.BlockSpec((1, tm, LANES), cur),
            pl.BlockSpec((1, BLOCK, LANES), prev),
            pl.BlockSpec((1, tm, d), post),
            pl.BlockSpec((1, 6, d), lambda t: (post_tile(t) // n, 0, 0)),
            _resident((1, d)),
            _resident(wo.shape), _resident(wg.shape), _resident(wu.shape), _resident(wd.shape),
        ],
        out_specs=pl.BlockSpec((1, tm, d), post),
        scratch_shapes=[pltpu.VMEM((tm, nq), BF16), pltpu.VMEM((tm, d), F32)],
        compiler_params=_params(("arbitrary",)),
        name="layer_a",
    )(sinks, q, k, k, v, v, x, mod, gain2, wo, wg, wu, wd)


def _post(o, x, wo, mod, gain2, wg, wu, wd):
    b, s, d = x.shape
    tm = ROW_TILE
    row = lambda n: pl.BlockSpec((1, tm, n), lambda bi, i: (bi, i, 0))
    return pl.pallas_call(
        _post_kernel,
        out_shape=jax.ShapeDtypeStruct((b, s, d), F32),
        grid=(b, s // tm),
        in_specs=[
            row(o.shape[-1]), row(d),
            pl.BlockSpec((1, 6, d), lambda bi, i: (bi, 0, 0)),
            _resident((1, d)),
            _resident(wo.shape), _resident(wg.shape), _resident(wu.shape), _resident(wd.shape),
        ],
        out_specs=row(d),
        scratch_shapes=[pltpu.VMEM((tm, d), F32)],
        compiler_params=_params(("parallel", "parallel")),
        name="post",
    )(o, x, mod, gain2, wo, wg, wu, wd)


def _pair_perm():
    cols = []
    for p in range(GROUP_A):
        for h in (p, p + GROUP_A):
            cols.extend(range(h * HEAD_DIM, (h + 1) * HEAD_DIM))
    return jnp.asarray(cols, jnp.int32)


def kernel(x, c, positions, ada_w, ada_b, norm1_g, norm2_g, wqkv_a, q_norm_a, k_norm_a, sinks_a,
           wo_a, wqkv_b, wo_b, w_gate, w_up, w_down):
    b, s, d = x.shape
    depth = ada_w.shape[0]
    assert w_gate.shape[-1] % FF_CHUNK == 0
    scale = HEAD_DIM ** -0.5

    mod = _adaln(c, ada_w, ada_b).reshape(depth, b, 6, d)

    perm = _pair_perm()
    nqa = N_Q_A * HEAD_DIM
    lane = jnp.arange(LANES)
    gsum = (lane[:, None] // HEAD_DIM == lane[None, :] // HEAD_DIM).astype(BF16)

    for i in range(depth):
        j = i // 2
        post_args = (mod[i], norm2_g[i].reshape(1, d), w_gate[i].astype(BF16), w_up[i].astype(BF16),
                     w_down[i].astype(BF16))
        if i % 2 == 0:
            w = wqkv_a[j]
            w = jnp.concatenate([w[:, :nqa][:, perm], w[:, nqa:]], axis=1).astype(BF16)
            gains = jnp.stack([jnp.tile(q_norm_a[j] * scale, 2), jnp.tile(k_norm_a[j], 2)])
            q, k, v = _proj_a(x, mod[i], norm1_g[i].reshape(1, d), w, gains, gsum, positions)
            x = _layer_a(sinks_a[j], q, k, v, x, wo_a[j][perm, :].astype(BF16), *post_args)
        else:
            w = wqkv_b[j]
            nqb = N_H_B * HEAD_DIM
            w = jnp.concatenate([w[:, :nqb] * (scale * LOG2E), w[:, nqb:]], axis=1).astype(BF16)
            o = _attn_b(_proj_b(x, mod[i], norm1_g[i].reshape(1, d), w))
            x = _post(o, x, wo_b[j].astype(BF16), *post_args)
    return x
```

```python
import functools

import jax
import jax.numpy as jnp
from jax import lax
from jax.experimental import pallas as pl
from jax.experimental.pallas import tpu as pltpu

F32 = jnp.float32
BF16 = jnp.bfloat16

HEAD_DIM = 64
N_Q_A = 16
N_KV_A = 2
GROUP_A = N_Q_A // N_KV_A
N_H_B = 16
BLOCK = 128
WINDOW = 128
assert WINDOW == BLOCK
ROT_DIM = HEAD_DIM // 4
ROPE_THETA = 500000.0
EPS = 1e-6
LANES = 128
FF_CHUNK = 256
ROW_TILE = 512
LAYER_B_ROWS = 1024
PROJ_A_ROWS = 1024
PROJ_A_SUB = 256
PROJ_B_SUB = 256
ATTN_A_BLOCKS = 4
LAYER_A_RATIO = 6
VMEM_LIMIT = 56 * 1024 * 1024
SB_CUTOFF = -104.0
SB_ROWS = 64
SB_KEYS = 384
SB_GROUP = 29
LOG2E = 1.4426950408889634

_NT = (((1,), (1,)), ((), ()))


def _params(sem):
    return pltpu.CompilerParams(dimension_semantics=sem, vmem_limit_bytes=VMEM_LIMIT)


def _resident(shape):
    return pl.BlockSpec(shape, lambda *_: (0,) * len(shape), pipeline_mode=pl.Buffered(1))


def _rms_mod(x, gain, scale, shift):
    ms = jnp.mean(x * x, axis=-1, keepdims=True)
    y = x * lax.rsqrt(ms + EPS)
    return (y * gain) * (1.0 + scale) + shift


def _split_dot(x, m):
    hi = x.astype(BF16)
    lo = (x - hi.astype(F32)).astype(BF16)
    return (jnp.dot(hi, m, preferred_element_type=F32)
            + jnp.dot(lo, m, preferred_element_type=F32))


def _adaln_kernel(c_ref, w_ref, b_ref, o_ref):
    c = c_ref[...]
    cond = (c * jax.nn.sigmoid(c)).astype(BF16)
    o_ref[0] = jnp.dot(cond, w_ref[0].astype(BF16), preferred_element_type=F32) + b_ref[0]


def _adaln(c, ada_w, ada_b):
    depth, d, n = ada_w.shape
    b = c.shape[0]
    tn = 1024
    return pl.pallas_call(
        _adaln_kernel,
        out_shape=jax.ShapeDtypeStruct((depth, b, n), F32),
        grid=(depth, n // tn),
        in_specs=[
            pl.BlockSpec((b, d), lambda l, j: (0, 0)),
            pl.BlockSpec((1, d, tn), lambda l, j: (l, 0, j)),
            pl.BlockSpec((1, 1, tn), lambda l, j: (l, 0, j)),
        ],
        out_specs=pl.BlockSpec((1, b, tn), lambda l, j: (l, 0, j)),
        compiler_params=_params(("arbitrary", "arbitrary")),
        name="adaln",
    )(c, ada_w, ada_b.reshape(depth, 1, n))


def _rope_consts():
    half = ROT_DIM // 2
    inv_freq = jnp.power(F32(ROPE_THETA), -jnp.arange(half, dtype=F32) * 2.0 / ROT_DIM)
    f = jnp.arange(half)[:, None]
    d = jnp.arange(LANES)[None, :] % HEAD_DIM
    zero = jnp.zeros((half, LANES), F32)
    from_cos = jnp.concatenate([((d == f) | (d == f + half)).astype(F32), zero, zero], axis=1)
    from_sin = jnp.concatenate([zero, -(d == f).astype(F32), (d == f + half).astype(F32)], axis=1)
    place = jnp.concatenate([from_cos] * 3 + [from_sin] * 3, axis=0).astype(BF16)
    base = (d >= ROT_DIM).astype(F32)
    return inv_freq.reshape(half, 1), place, base


def _split3(x):
    hi = x.astype(BF16).astype(F32)
    mid = (x - hi).astype(BF16).astype(F32)
    return [hi, mid, x - hi - mid]


def _proj_a_kernel(x_ref, mod_ref, g_ref, w_ref, gain_ref, gsum_ref, pos_ref, invf_ref, place_ref,
                   base_ref, q_ref, k_ref, v_ref):
    ang = pos_ref[0].astype(F32) * invf_ref[...]
    trig = jnp.concatenate(_split3(jnp.cos(ang)) + _split3(jnp.sin(ang)), axis=0).astype(BF16)
    gsum = gsum_ref[...]
    half = ROT_DIM // 2
    nq = q_ref.shape[-1] // LANES

    def project(rows):
        h = _rms_mod(x_ref[0, rows, :], g_ref[...], mod_ref[0, 1:2, :], mod_ref[0, 0:1, :])
        return jnp.dot(h.astype(BF16), w_ref[...], preferred_element_type=F32)

    def finish(rows, qkv):
        tables = lax.dot_general(trig[:, rows], place_ref[...], (((0,), (0,)), ((), ())),
                                 preferred_element_type=F32)
        cos = tables[:, :LANES] + base_ref[...]
        sup, sdn = tables[:, LANES:2 * LANES], tables[:, 2 * LANES:]

        def norm_rope(blk, gain):
            ssq = _split_dot(blk * blk, gsum)
            y = blk * lax.rsqrt(ssq * (1.0 / HEAD_DIM) + EPS) * gain
            return y * cos + pltpu.roll(y, LANES - half, 1) * sup + pltpu.roll(y, half, 1) * sdn

        for cb in range(nq):
            blk = qkv[:, cb * LANES:(cb + 1) * LANES]
            q_ref[0, rows, cb * LANES:(cb + 1) * LANES] = norm_rope(
                blk, gain_ref[0:1, :]).astype(BF16)
        k_ref[0, rows, :] = norm_rope(qkv[:, nq * LANES:(nq + 1) * LANES],
                                      gain_ref[1:2, :]).astype(BF16)
        v_ref[0, rows, :] = qkv[:, (nq + 1) * LANES:(nq + 2) * LANES].astype(BF16)

    subs = [slice(r, r + PROJ_A_SUB) for r in range(0, x_ref.shape[1], PROJ_A_SUB)]
    pending = None
    for rows in subs:
        qkv = project(rows)
        if pending is not None:
            finish(*pending)
        pending = (rows, qkv)
    finish(*pending)


def _proj_a(x, mod, gain1, w, gains, gsum, positions):
    b, s, d = x.shape
    tm = PROJ_A_ROWS
    nq = N_Q_A * HEAD_DIM
    inv_freq, place, base = _rope_consts()
    row = lambda n: pl.BlockSpec((1, tm, n), lambda bi, i: (bi, i, 0))
    out = lambda n: jax.ShapeDtypeStruct((b, s, n), BF16)
    return pl.pallas_call(
        _proj_a_kernel,
        out_shape=(out(nq), out(LANES), out(LANES)),
        grid=(b, s // tm),
        in_specs=[
            row(d),
            pl.BlockSpec((1, 6, d), lambda bi, i: (bi, 0, 0)),
            _resident((1, d)),
            _resident(w.shape),
            _resident(gains.shape),
            _resident(gsum.shape),
            pl.BlockSpec((1, 1, tm), lambda bi, i: (bi, 0, i)),
            _resident(inv_freq.shape), _resident(place.shape), _resident(base.shape),
        ],
        out_specs=(row(nq), row(LANES), row(LANES)),
        compiler_params=_params(("parallel", "parallel")),
        name="proj_a",
    )(x, mod, gain1, w, gains, gsum, positions.reshape(b, 1, s), inv_freq, place, base)


def _attn_a_steps(first, sink_ref, q_ref, kc_ref, kp_ref, vc_ref, vp_ref, store):
    lane2 = lax.broadcasted_iota(jnp.int32, (2 * BLOCK, LANES), 1)
    row2 = lax.broadcasted_iota(jnp.int32, (2 * BLOCK, LANES), 0)
    keep = (lane2 < HEAD_DIM) == (row2 < BLOCK)
    own = lane2 <= (row2 & (BLOCK - 1))
    rowc = lax.broadcasted_iota(jnp.int32, (2 * BLOCK, 1), 0)
    lane1 = lax.broadcasted_iota(jnp.int32, (BLOCK, LANES), 1)
    sinks = [jnp.where(rowc < BLOCK, sink_ref[p], sink_ref[p + GROUP_A]) for p in range(GROUP_A)]

    def keys_values(jb):
        rows = slice(jb * BLOCK, (jb + 1) * BLOCK)
        before = slice((jb - 1) * BLOCK, jb * BLOCK)
        kprev, vprev = (kp_ref[0], vp_ref[0]) if jb == 0 else (kc_ref[0, before, :], vc_ref[0, before, :])
        return (jnp.concatenate([kprev, kc_ref[0, rows, :]], axis=0),
                jnp.concatenate([vprev, vc_ref[0, rows, :]], axis=0))

    def all_scores(jb, kcat):
        lhs = []
        for p in range(GROUP_A):
            qp = q_ref[0, jb * BLOCK:(jb + 1) * BLOCK, p * LANES:(p + 1) * LANES]
            qq = jnp.concatenate([qp, qp], axis=0)
            lhs.append(jnp.where(keep, qq, jnp.zeros_like(qq)))
        s = lax.dot_general(jnp.concatenate(lhs, axis=0), kcat, _NT, preferred_element_type=F32)
        return [s[p * 2 * BLOCK:(p + 1) * 2 * BLOCK] for p in range(GROUP_A)]

    nblk = q_ref.shape[1] // BLOCK
    kv = keys_values(0)
    scores = all_scores(0, kv[0])
    yield
    for jb in range(nblk):
        vcat = kv[1]
        no_prev = jnp.where(first, -1e30, 0.0).astype(F32) if jb == 0 else None
        weights, denoms = [], []
        for p in range(GROUP_A):
            before = scores[p][:, :BLOCK]
            s = jnp.where(own, scores[p][:, BLOCK:], before if no_prev is None else before + no_prev)
            m = jnp.maximum(jnp.max(s, axis=-1, keepdims=True), sinks[p])
            e = jnp.exp(s - m)
            denoms.append(jnp.sum(e, axis=-1, keepdims=True) + jnp.exp(sinks[p] - m))
            e = e.astype(BF16)
            zero = jnp.zeros_like(e)
            weights.append(jnp.concatenate([jnp.where(own, zero, e), jnp.where(own, e, zero)], axis=1))
        if jb + 1 < nblk:
            kv = keys_values(jb + 1)
            scores = all_scores(jb + 1, kv[0])
            yield
        pvs = jnp.dot(jnp.concatenate(weights, axis=0), vcat, preferred_element_type=F32)
        for p in range(GROUP_A):
            pv = pvs[p * 2 * BLOCK:(p + 1) * 2 * BLOCK] * (1.0 / denoms[p])
            store(jb, p, jnp.where(lane1 < HEAD_DIM, pv[:BLOCK], pv[BLOCK:]).astype(BF16))
        yield


def _proj_b_kernel(x_ref, mod_ref, g_ref, w_ref, o_ref):
    d = w_ref.shape[0]
    for r in range(0, x_ref.shape[1], PROJ_B_SUB):
        rows = slice(r, r + PROJ_B_SUB)
        h = _rms_mod(x_ref[0, rows, :], g_ref[...], mod_ref[0, 1:2, :],
                     mod_ref[0, 0:1, :]).astype(BF16)
        for n in range(w_ref.shape[1] // d):
            o_ref[0, rows, n * d:(n + 1) * d] = jnp.dot(
                h, w_ref[:, n * d:(n + 1) * d], preferred_element_type=F32).astype(BF16)


def _proj_b(x, mod, gain1, w):
    b, s, d = x.shape
    tm = LAYER_B_ROWS
    n = w.shape[1]
    return pl.pallas_call(
        _proj_b_kernel,
        out_shape=jax.ShapeDtypeStruct((b, s, n), BF16),
        grid=(b, s // tm),
        in_specs=[
            pl.BlockSpec((1, tm, d), lambda bi, i: (bi, i, 0)),
            pl.BlockSpec((1, 6, d), lambda bi, i: (bi, 0, 0)),
            _resident((1, d)),
            _resident(w.shape),
        ],
        out_specs=pl.BlockSpec((1, tm, n), lambda bi, i: (bi, i, 0)),
        compiler_params=_params(("parallel", "parallel")),
        name="proj_b",
    )(x, mod, gain1, w)


def _attn_b_kernel(q_ref, k_ref, v_ref, uo_ref, o_ref):
    uo = uo_ref[...]
    nrow = 2 * SB_ROWS
    lane = lax.broadcasted_iota(jnp.int32, (nrow, LANES), 1)
    row = lax.broadcasted_iota(jnp.int32, (nrow, LANES), 0)
    keep = (lane < HEAD_DIM) == (row < SB_ROWS)
    col_minus_row = lane - (row & (SB_ROWS - 1))
    out_head0 = lax.broadcasted_iota(jnp.int32, (SB_ROWS, LANES), 1) < HEAD_DIM
    cutoff = SB_CUTOFF * LOG2E
    sign = jnp.int32(-2 ** 31)

    def stacked_queries(t0):
        q = q_ref[0, pl.ds(t0, SB_ROWS), :]
        qq = jnp.concatenate([q, q], axis=0)
        return jnp.where(keep, qq, jnp.zeros_like(qq))

    def scores(lhs, start, nkeys):
        return lax.dot_general(lhs, k_ref[0, pl.ds(start, nkeys), :], _NT,
                               preferred_element_type=F32)

    def log_pieces(z, mask):
        neg_abs = lax.bitcast_convert_type(lax.bitcast_convert_type(z, jnp.int32) | sign, F32)
        log_beta = jnp.minimum(z, 0.0) - jnp.log2(1.0 + jnp.exp2(neg_abs))
        log_rest = log_beta - z
        if mask is not None:
            log_rest = jnp.where(mask, log_rest, 0.0)
        hi = log_rest.astype(BF16)
        lo = (log_rest - hi.astype(F32)).astype(BF16)
        return log_beta, jnp.concatenate([hi, lo], axis=1)

    def tile_sums(pieces):
        sums = jnp.dot(jnp.concatenate(pieces, axis=0), uo, preferred_element_type=F32)
        return [sums[i * nrow:(i + 1) * nrow] for i in range(len(pieces))]

    def weights(log_beta, sums, later, mask):
        t = log_beta + sums[:, :BLOCK]
        if later is not None:
            t = t + later
        a = jnp.exp2(t)
        if mask is not None:
            a = jnp.where(mask, a, 0.0)
        return a.astype(BF16)

    def store(t0, pv):
        o_ref[0, pl.ds(t0, SB_ROWS), :] = jnp.where(
            out_head0, pv[:SB_ROWS], pv[SB_ROWS:]).astype(BF16)

    def first_windows(units):
        tiles, zs, vals = [], [], []
        for t0, start, limit in units:
            mine = [(b, None if b * BLOCK + BLOCK - 1 < limit else col_minus_row < limit - b * BLOCK)
                    for b in range(SB_KEYS // BLOCK) if b * BLOCK - (SB_ROWS - 1) < limit]
            tiles.append(mine)
            zs.append(scores(stacked_queries(t0), start, len(mine) * BLOCK))
            vals.append(v_ref[0, pl.ds(start, len(mine) * BLOCK), :])
        terms = [dict() for _ in units]
        for b in reversed(range(SB_KEYS // BLOCK)):
            have = [(u, mask) for u, mine in enumerate(tiles) for tb, mask in mine if tb == b]
            parts = [log_pieces(zs[u][:, b * BLOCK:(b + 1) * BLOCK], mask) for u, mask in have]
            for (u, _), (log_beta, _), sums in zip(have, parts, tile_sums([p for _, p in parts])):
                terms[u][b] = (log_beta, sums)
        out = []
        for u, (t0, _, _) in enumerate(units):
            later, ws = None, []
            for b, mask in reversed(tiles[u]):
                log_beta, sums = terms[u][b]
                ws.insert(0, weights(log_beta, sums, later, mask))
                later = sums[:, BLOCK:] if later is None else later + sums[:, BLOCK:]
            pv = jnp.dot(jnp.concatenate(ws, axis=1), vals[u], preferred_element_type=F32)
            store(t0, pv)
            out.append((later, pv))
        return out

    def sweep_earlier(t0, start, total, pv):
        lhs = stacked_queries(t0)

        def cond(st):
            return (st[0] > 0) & (st[1] > 0)

        def body(st):
            end, _, carry, acc = st
            begin = pl.multiple_of(jnp.maximum(end - BLOCK, 0), SB_ROWS)
            mask = lane < end - begin
            log_beta, pieces = log_pieces(scores(lhs, begin, BLOCK), mask)
            sums, = tile_sums([pieces])
            w = weights(log_beta, sums, carry, mask)
            acc = acc + jnp.dot(w, v_ref[0, pl.ds(begin, BLOCK), :], preferred_element_type=F32)
            carry = carry + sums[:, BLOCK:]
            return begin, (jnp.max(carry) > cutoff).astype(jnp.int32), carry, acc

        st = lax.while_loop(cond, body, (jnp.int32(start), jnp.int32(1), total, pv))
        store(t0, st[3])

    def run_units(units):
        res = first_windows(units)
        check = [(u, r) for u, r in zip(units, res) if not (isinstance(u[1], int) and u[1] == 0)]
        if not check:
            return
        flags = [(jnp.max(total) > cutoff).astype(jnp.int32) for _, (total, _) in check]

        @pl.when(functools.reduce(jnp.maximum, flags) > 0)
        def _():
            for ((t0, start, _), (total, pv)), flag in zip(check, flags):
                pl.when(flag > 0)(functools.partial(sweep_earlier, t0, start, total, pv))

    nunit = q_ref.shape[1] // SB_ROWS
    reach = SB_KEYS - SB_ROWS
    nclamp = reach // SB_ROWS
    npeel = nclamp + (nunit - nclamp) % SB_GROUP
    run_units([(u * SB_ROWS, max(u * SB_ROWS - reach, 0), min(u * SB_ROWS, reach))
               for u in range(npeel)])

    def group(g, _):
        base = (npeel + g * SB_GROUP) * SB_ROWS
        t0s = [pl.multiple_of(base + m * SB_ROWS, SB_ROWS) for m in range(SB_GROUP)]
        run_units([(t0, pl.multiple_of(t0 - reach, SB_ROWS), reach) for t0 in t0s])
        return 0

    lax.fori_loop(0, (nunit - npeel) // SB_GROUP, group, 0)


def _attn_b(qkv):
    b, s, n = qkv.shape
    npair = n // 3 // LANES
    assert s % SB_ROWS == 0 and s >= SB_KEYS
    key = jnp.arange(BLOCK)
    later = (key[:, None] > key[None, :]).astype(BF16)
    uo = jnp.concatenate([later, jnp.ones((BLOCK, BLOCK), BF16)], axis=1)
    uo = jnp.concatenate([uo, uo], axis=0)
    blk = lambda off: pl.BlockSpec((1, s, LANES), lambda bi, hp: (bi, 0, off + hp))
    return pl.pallas_call(
        _attn_b_kernel,
        out_shape=jax.ShapeDtypeStruct((b, s, n // 3), BF16),
        grid=(b, npair),
        in_specs=[blk(0), blk(npair), blk(2 * npair), _resident(uo.shape)],
        out_specs=blk(0),
        compiler_params=_params(("parallel", "parallel")),
        name="attn_b",
    )(qkv, qkv, qkv, uo)


def _post_steps(o, x_ref, mod_ref, g_ref, wo_ref, wg_ref, wu_ref, wd_ref, out_ref, acc_ref):
    y = jnp.dot(o, wo_ref[...], preferred_element_type=F32)
    yield
    x1 = x_ref[0] + mod_ref[0, 2:3, :] * y
    h = _rms_mod(x1, g_ref[...], mod_ref[0, 4:5, :], mod_ref[0, 3:4, :]).astype(BF16)
    for f in range(wg_ref.shape[1] // FF_CHUNK):
        cols = slice(f * FF_CHUNK, (f + 1) * FF_CHUNK)
        g = jnp.dot(h, wg_ref[:, cols], preferred_element_type=F32)
        yield
        u = jnp.dot(h, wu_ref[:, cols], preferred_element_type=F32)
        yield
        act = ((g * jax.nn.sigmoid(g)) * u).astype(BF16)
        down = jnp.dot(act, wd_ref[cols, :], preferred_element_type=F32)
        if f == 0:
            acc_ref[...] = down
        else:
            acc_ref[...] += down
        yield
    out_ref[0] = x1 + mod_ref[0, 5:6, :] * acc_ref[...]


def _post_kernel(o_ref, *refs):
    for _ in _post_steps(o_ref[0], *refs):
        pass


def _layer_a_kernel(tiles_per_seq, sink_ref, q_ref, kc_ref, kp_ref, vc_ref, vp_ref, x_ref, mod_ref,
                    g_ref, wo_ref, wg_ref, wu_ref, wd_ref, out_ref, o_ref, acc_ref):
    t = pl.program_id(0)

    @pl.when(t == 0)
    def _():
        o_ref[...] = jnp.zeros_like(o_ref)

    post = _post_steps(o_ref[...], x_ref, mod_ref, g_ref, wo_ref, wg_ref, wu_ref, wd_ref, out_ref,
                       acc_ref)
    next(post)

    def store(jb, p, tile):
        o_ref[jb * BLOCK:(jb + 1) * BLOCK, p * LANES:(p + 1) * LANES] = tile

    last = pl.num_programs(0) - 2
    first = jnp.minimum(t, last) % tiles_per_seq == 0
    attn = _attn_a_steps(first, sink_ref, q_ref, kc_ref, kp_ref, vc_ref, vp_ref, store)
    done = object()
    running = True
    while running:
        running = next(attn, done) is not done
        for _ in range(LAYER_A_RATIO):
            running |= next(post, done) is not done


def _layer_a(sinks, q, k, v, x, wo, mod, gain2, wg, wu, wd):
    b, s, d = x.shape
    nq = q.shape[-1]
    tm = ROW_TILE
    assert tm == ATTN_A_BLOCKS * BLOCK
    n = s // tm
    ntile = b * n
    attn_tile = lambda t: jnp.minimum(t, ntile - 1)
    post_tile = lambda t: jnp.maximum(t - 1, 0)
    cur = lambda t: (attn_tile(t) // n, attn_tile(t) % n, 0)
    prev = lambda t: (attn_tile(t) // n, jnp.maximum((attn_tile(t) % n) * ATTN_A_BLOCKS - 1, 0), 0)
    post = lambda t: (post_tile(t) // n, post_tile(t) % n, 0)
    return pl.pallas_call(
        functools.partial(_layer_a_kernel, n),
        out_shape=jax.ShapeDtypeStruct((b, s, d), F32),
        grid=(ntile + 1,),
        in_specs=[
            pl.BlockSpec(memory_space=pltpu.SMEM),
            pl.BlockSpec((1, tm, nq), cur),
            pl.BlockSpec((1, tm, LANES), cur),
            pl.BlockSpec((1, BLOCK, LANES), prev),
            pl.BlockSpec((1, tm, LANES), cur),
            pl.BlockSpec((1, BLOCK, LANES), prev),
            pl.BlockSpec((1, tm, d), post),
            pl.BlockSpec((1, 6, d), lambda t: (post_tile(t) // n, 0, 0)),
            _resident((1, d)),
            _resident(wo.shape), _resident(wg.shape), _resident(wu.shape), _resident(wd.shape),
        ],
        out_specs=pl.BlockSpec((1, tm, d), post),
        scratch_shapes=[pltpu.VMEM((tm, nq), BF16), pltpu.VMEM((tm, d), F32)],
        compiler_params=_params(("arbitrary",)),
        name="layer_a",
    )(sinks, q, k, k, v, v, x, mod, gain2, wo, wg, wu, wd)


def _post(o, x, wo, mod, gain2, wg, wu, wd):
    b, s, d = x.shape
    tm = LAYER_B_ROWS
    row = lambda n: pl.BlockSpec((1, tm, n), lambda bi, i: (bi, i, 0))
    return pl.pallas_call(
        _post_kernel,
        out_shape=jax.ShapeDtypeStruct((b, s, d), F32),
        grid=(b, s // tm),
        in_specs=[
            row(o.shape[-1]), row(d),
            pl.BlockSpec((1, 6, d), lambda bi, i: (bi, 0, 0)),
            _resident((1, d)),
            _resident(wo.shape), _resident(wg.shape), _resident(wu.shape), _resident(wd.shape),
        ],
        out_specs=row(d),
        scratch_shapes=[pltpu.VMEM((tm, d), F32)],
        compiler_params=_params(("parallel", "parallel")),
        name="post",
    )(o, x, mod, gain2, wo, wg, wu, wd)


def _pair_perm():
    cols = []
    for p in range(GROUP_A):
        for h in (p, p + GROUP_A):
            cols.extend(range(h * HEAD_DIM, (h + 1) * HEAD_DIM))
    return jnp.asarray(cols, jnp.int32)


def kernel(x, c, positions, ada_w, ada_b, norm1_g, norm2_g, wqkv_a, q_norm_a, k_norm_a, sinks_a,
           wo_a, wqkv_b, wo_b, w_gate, w_up, w_down):
    b, s, d = x.shape
    depth = ada_w.shape[0]
    assert w_gate.shape[-1] % FF_CHUNK == 0
    scale = HEAD_DIM ** -0.5

    mod = _adaln(c, ada_w, ada_b).reshape(depth, b, 6, d)

    perm = _pair_perm()
    nqa = N_Q_A * HEAD_DIM
    lane = jnp.arange(LANES)
    gsum = (lane[:, None] // HEAD_DIM == lane[None, :] // HEAD_DIM).astype(BF16)

    for i in range(depth):
        j = i // 2
        post_args = (mod[i], norm2_g[i].reshape(1, d), w_gate[i].astype(BF16), w_up[i].astype(BF16),
                     w_down[i].astype(BF16))
        if i % 2 == 0:
            w = wqkv_a[j]
            w = jnp.concatenate([w[:, :nqa][:, perm], w[:, nqa:]], axis=1).astype(BF16)
            gains = jnp.stack([jnp.tile(q_norm_a[j] * scale, 2), jnp.tile(k_norm_a[j], 2)])
            q, k, v = _proj_a(x, mod[i], norm1_g[i].reshape(1, d), w, gains, gsum, positions)
            x = _layer_a(sinks_a[j], q, k, v, x, wo_a[j][perm, :].astype(BF16), *post_args)
        else:
            w = wqkv_b[j]
            nqb = N_H_B * HEAD_DIM
            w = jnp.concatenate([w[:, :nqb] * (scale * LOG2E), w[:, nqb:]], axis=1).astype(BF16)
            o = _attn_b(_proj_b(x, mod[i], norm1_g[i].reshape(1, d), w))
            x = _post(o, x, wo_b[j].astype(BF16), *post_args)
    return x
```

```python
import functools

import jax
import jax.numpy as jnp
from jax import lax
from jax.experimental import pallas as pl
from jax.experimental.pallas import tpu as pltpu

F32 = jnp.float32
BF16 = jnp.bfloat16

HEAD_DIM = 64
N_Q_A = 16
N_KV_A = 2
GROUP_A = N_Q_A // N_KV_A
N_H_B = 16
BLOCK = 128
WINDOW = 128
assert WINDOW == BLOCK
ROT_DIM = HEAD_DIM // 4
ROPE_THETA = 500000.0
EPS = 1e-6
LANES = 128
FF_CHUNK = 256
ROW_TILE = 512
LAYER_B_ROWS = 1024
PROJ_A_ROWS = 2048
PROJ_A_SUB = 256
PROJ_B_SUB = 256
ATTN_A_BLOCKS = 4
LAYER_A_RATIO = 6
VMEM_LIMIT = 56 * 1024 * 1024
SB_CUTOFF = -104.0
SB_ROWS = 64
SB_KEYS = 384
SB_GROUP = 59
LOG2E = 1.4426950408889634

_NT = (((1,), (1,)), ((), ()))


def _params(sem):
    return pltpu.CompilerParams(dimension_semantics=sem, vmem_limit_bytes=VMEM_LIMIT)


def _resident(shape):
    return pl.BlockSpec(shape, lambda *_: (0,) * len(shape), pipeline_mode=pl.Buffered(1))


def _rms_mod(x, gain, scale, shift):
    ms = jnp.mean(x * x, axis=-1, keepdims=True)
    y = x * lax.rsqrt(ms + EPS)
    return (y * gain) * (1.0 + scale) + shift


def _split_dot(x, m):
    hi = x.astype(BF16)
    lo = (x - hi.astype(F32)).astype(BF16)
    return (jnp.dot(hi, m, preferred_element_type=F32)
            + jnp.dot(lo, m, preferred_element_type=F32))


def _adaln_kernel(c_ref, w_ref, b_ref, o_ref):
    c = c_ref[...]
    cond = (c * jax.nn.sigmoid(c)).astype(BF16)
    o_ref[0] = jnp.dot(cond, w_ref[0].astype(BF16), preferred_element_type=F32) + b_ref[0]


def _adaln(c, ada_w, ada_b):
    depth, d, n = ada_w.shape
    b = c.shape[0]
    tn = 1024
    return pl.pallas_call(
        _adaln_kernel,
        out_shape=jax.ShapeDtypeStruct((depth, b, n), F32),
        grid=(depth, n // tn),
        in_specs=[
            pl.BlockSpec((b, d), lambda l, j: (0, 0)),
            pl.BlockSpec((1, d, tn), lambda l, j: (l, 0, j)),
            pl.BlockSpec((1, 1, tn), lambda l, j: (l, 0, j)),
        ],
        out_specs=pl.BlockSpec((1, b, tn), lambda l, j: (l, 0, j)),
        compiler_params=_params(("arbitrary", "arbitrary")),
        name="adaln",
    )(c, ada_w, ada_b.reshape(depth, 1, n))


def _rope_consts():
    half = ROT_DIM // 2
    inv_freq = jnp.power(F32(ROPE_THETA), -jnp.arange(half, dtype=F32) * 2.0 / ROT_DIM)
    f = jnp.arange(half)[:, None]
    d = jnp.arange(LANES)[None, :] % HEAD_DIM
    zero = jnp.zeros((half, LANES), F32)
    from_cos = jnp.concatenate([((d == f) | (d == f + half)).astype(F32), zero, zero], axis=1)
    from_sin = jnp.concatenate([zero, -(d == f).astype(F32), (d == f + half).astype(F32)], axis=1)
    place = jnp.concatenate([from_cos] * 3 + [from_sin] * 3, axis=0).astype(BF16)
    base = (d >= ROT_DIM).astype(F32)
    return inv_freq.reshape(half, 1), place, base


def _split3(x):
    hi = x.astype(BF16).astype(F32)
    mid = (x - hi).astype(BF16).astype(F32)
    return [hi, mid, x - hi - mid]


def _proj_a_kernel(x_ref, mod_ref, g_ref, w_ref, gain_ref, gsum_ref, pos_ref, invf_ref, place_ref,
                   base_ref, q_ref, k_ref, v_ref):
    ang = pos_ref[0].astype(F32) * invf_ref[...]
    trig = jnp.concatenate(_split3(jnp.cos(ang)) + _split3(jnp.sin(ang)), axis=0).astype(BF16)
    gsum = gsum_ref[...]
    half = ROT_DIM // 2
    nq = q_ref.shape[-1] // LANES

    def project(rows):
        h = _rms_mod(x_ref[0, rows, :], g_ref[...], mod_ref[0, 1:2, :], mod_ref[0, 0:1, :])
        return jnp.dot(h.astype(BF16), w_ref[...], preferred_element_type=F32)

    def finish(rows, qkv):
        tables = lax.dot_general(trig[:, rows], place_ref[...], (((0,), (0,)), ((), ())),
                                 preferred_element_type=F32)
        cos = tables[:, :LANES] + base_ref[...]
        sup, sdn = tables[:, LANES:2 * LANES], tables[:, 2 * LANES:]

        def norm_rope(blk, gain):
            ssq = _split_dot(blk * blk, gsum)
            y = blk * lax.rsqrt(ssq * (1.0 / HEAD_DIM) + EPS) * gain
            return y * cos + pltpu.roll(y, LANES - half, 1) * sup + pltpu.roll(y, half, 1) * sdn

        for cb in range(nq):
            blk = qkv[:, cb * LANES:(cb + 1) * LANES]
            q_ref[0, rows, cb * LANES:(cb + 1) * LANES] = norm_rope(
                blk, gain_ref[0:1, :]).astype(BF16)
        k_ref[0, rows, :] = norm_rope(qkv[:, nq * LANES:(nq + 1) * LANES],
                                      gain_ref[1:2, :]).astype(BF16)
        v_ref[0, rows, :] = qkv[:, (nq + 1) * LANES:(nq + 2) * LANES].astype(BF16)

    subs = [slice(r, r + PROJ_A_SUB) for r in range(0, x_ref.shape[1], PROJ_A_SUB)]
    pending = None
    for rows in subs:
        qkv = project(rows)
        if pending is not None:
            finish(*pending)
        pending = (rows, qkv)
    finish(*pending)


def _proj_a(x, mod, gain1, w, gains, gsum, positions):
    b, s, d = x.shape
    tm = PROJ_A_ROWS
    nq = N_Q_A * HEAD_DIM
    inv_freq, place, base = _rope_consts()
    row = lambda n: pl.BlockSpec((1, tm, n), lambda bi, i: (bi, i, 0))
    out = lambda n: jax.ShapeDtypeStruct((b, s, n), BF16)
    return pl.pallas_call(
        _proj_a_kernel,
        out_shape=(out(nq), out(LANES), out(LANES)),
        grid=(b, s // tm),
        in_specs=[
            row(d),
            pl.BlockSpec((1, 6, d), lambda bi, i: (bi, 0, 0)),
            _resident((1, d)),
            _resident(w.shape),
            _resident(gains.shape),
            _resident(gsum.shape),
            pl.BlockSpec((1, 1, tm), lambda bi, i: (bi, 0, i)),
            _resident(inv_freq.shape), _resident(place.shape), _resident(base.shape),
        ],
        out_specs=(row(nq), row(LANES), row(LANES)),
        compiler_params=_params(("parallel", "parallel")),
        name="proj_a",
    )(x, mod, gain1, w, gains, gsum, positions.reshape(b, 1, s), inv_freq, place, base)


def _attn_a_steps(first, sink_ref, q_ref, kc_ref, kp_ref, vc_ref, vp_ref, store):
    lane2 = lax.broadcasted_iota(jnp.int32, (2 * BLOCK, LANES), 1)
    row2 = lax.broadcasted_iota(jnp.int32, (2 * BLOCK, LANES), 0)
    keep = (lane2 < HEAD_DIM) == (row2 < BLOCK)
    own = lane2 <= (row2 & (BLOCK - 1))
    rowc = lax.broadcasted_iota(jnp.int32, (2 * BLOCK, 1), 0)
    lane1 = lax.broadcasted_iota(jnp.int32, (BLOCK, LANES), 1)
    sinks = [jnp.where(rowc < BLOCK, sink_ref[p], sink_ref[p + GROUP_A]) for p in range(GROUP_A)]

    def keys_values(jb):
        rows = slice(jb * BLOCK, (jb + 1) * BLOCK)
        before = slice((jb - 1) * BLOCK, jb * BLOCK)
        kprev, vprev = (kp_ref[0], vp_ref[0]) if jb == 0 else (kc_ref[0, before, :], vc_ref[0, before, :])
        return (jnp.concatenate([kprev, kc_ref[0, rows, :]], axis=0),
                jnp.concatenate([vprev, vc_ref[0, rows, :]], axis=0))

    def all_scores(jb, kcat):
        lhs = []
        for p in range(GROUP_A):
            qp = q_ref[0, jb * BLOCK:(jb + 1) * BLOCK, p * LANES:(p + 1) * LANES]
            qq = jnp.concatenate([qp, qp], axis=0)
            lhs.append(jnp.where(keep, qq, jnp.zeros_like(qq)))
        s = lax.dot_general(jnp.concatenate(lhs, axis=0), kcat, _NT, preferred_element_type=F32)
        return [s[p * 2 * BLOCK:(p + 1) * 2 * BLOCK] for p in range(GROUP_A)]

    nblk = q_ref.shape[1] // BLOCK
    kv = keys_values(0)
    scores = all_scores(0, kv[0])
    yield
    for jb in range(nblk):
        vcat = kv[1]
        no_prev = jnp.where(first, -1e30, 0.0).astype(F32) if jb == 0 else None
        weights, denoms = [], []
        for p in range(GROUP_A):
            before = scores[p][:, :BLOCK]
            s = jnp.where(own, scores[p][:, BLOCK:], before if no_prev is None else before + no_prev)
            m = jnp.maximum(jnp.max(s, axis=-1, keepdims=True), sinks[p])
            e = jnp.exp(s - m)
            denoms.append(jnp.sum(e, axis=-1, keepdims=True) + jnp.exp(sinks[p] - m))
            e = e.astype(BF16)
            zero = jnp.zeros_like(e)
            weights.append(jnp.concatenate([jnp.where(own, zero, e), jnp.where(own, e, zero)], axis=1))
        if jb + 1 < nblk:
            kv = keys_values(jb + 1)
            scores = all_scores(jb + 1, kv[0])
            yield
        pvs = jnp.dot(jnp.concatenate(weights, axis=0), vcat, preferred_element_type=F32)
        for p in range(GROUP_A):
            pv = pvs[p * 2 * BLOCK:(p + 1) * 2 * BLOCK] * (1.0 / denoms[p])
            store(jb, p, jnp.where(lane1 < HEAD_DIM, pv[:BLOCK], pv[BLOCK:]).astype(BF16))
        yield


def _proj_b_kernel(x_ref, mod_ref, g_ref, w_ref, o_ref):
    d = w_ref.shape[0]
    for r in range(0, x_ref.shape[1], PROJ_B_SUB):
        rows = slice(r, r + PROJ_B_SUB)
        h = _rms_mod(x_ref[0, rows, :], g_ref[...], mod_ref[0, 1:2, :],
                     mod_ref[0, 0:1, :]).astype(BF16)
        for n in range(w_ref.shape[1] // d):
            o_ref[0, rows, n * d:(n + 1) * d] = jnp.dot(
                h, w_ref[:, n * d:(n + 1) * d], preferred_element_type=F32).astype(BF16)


def _proj_b(x, mod, gain1, w):
    b, s, d = x.shape
    tm = LAYER_B_ROWS
    n = w.shape[1]
    return pl.pallas_call(
        _proj_b_kernel,
        out_shape=jax.ShapeDtypeStruct((b, s, n), BF16),
        grid=(b, s // tm),
        in_specs=[
            pl.BlockSpec((1, tm, d), lambda bi, i: (bi, i, 0)),
            pl.BlockSpec((1, 6, d), lambda bi, i: (bi, 0, 0)),
            _resident((1, d)),
            _resident(w.shape),
        ],
        out_specs=pl.BlockSpec((1, tm, n), lambda bi, i: (bi, i, 0)),
        compiler_params=_params(("parallel", "parallel")),
        name="proj_b",
    )(x, mod, gain1, w)


def _attn_b_kernel(q_ref, k_ref, v_ref, uo_ref, o_ref):
    uo = uo_ref[...]
    nrow = 2 * SB_ROWS
    lane = lax.broadcasted_iota(jnp.int32, (nrow, LANES), 1)
    row = lax.broadcasted_iota(jnp.int32, (nrow, LANES), 0)
    keep = (lane < HEAD_DIM) == (row < SB_ROWS)
    col_minus_row = lane - (row & (SB_ROWS - 1))
    out_head0 = lax.broadcasted_iota(jnp.int32, (SB_ROWS, LANES), 1) < HEAD_DIM
    cutoff = SB_CUTOFF * LOG2E
    sign = jnp.int32(-2 ** 31)

    def stacked_queries(t0):
        q = q_ref[0, pl.ds(t0, SB_ROWS), :]
        qq = jnp.concatenate([q, q], axis=0)
        return jnp.where(keep, qq, jnp.zeros_like(qq))

    def scores(lhs, start, nkeys):
        return lax.dot_general(lhs, k_ref[0, pl.ds(start, nkeys), :], _NT,
                               preferred_element_type=F32)

    def log_pieces(z, mask):
        neg_abs = lax.bitcast_convert_type(lax.bitcast_convert_type(z, jnp.int32) | sign, F32)
        log_beta = jnp.minimum(z, 0.0) - jnp.log2(1.0 + jnp.exp2(neg_abs))
        log_rest = log_beta - z
        if mask is not None:
            log_rest = jnp.where(mask, log_rest, 0.0)
        hi = log_rest.astype(BF16)
        lo = (log_rest - hi.astype(F32)).astype(BF16)
        return log_beta, jnp.concatenate([hi, lo], axis=1)

    def tile_sums(pieces):
        sums = jnp.dot(jnp.concatenate(pieces, axis=0), uo, preferred_element_type=F32)
        return [sums[i * nrow:(i + 1) * nrow] for i in range(len(pieces))]

    def weights(log_beta, sums, later, mask):
        t = log_beta + sums[:, :BLOCK]
        if later is not None:
            t = t + later
        a = jnp.exp2(t)
        if mask is not None:
            a = jnp.where(mask, a, 0.0)
        return a.astype(BF16)

    def store(t0, pv):
        o_ref[0, pl.ds(t0, SB_ROWS), :] = jnp.where(
            out_head0, pv[:SB_ROWS], pv[SB_ROWS:]).astype(BF16)

    def first_windows(units):
        tiles, zs, vals = [], [], []
        for t0, start, limit in units:
            mine = [(b, None if b * BLOCK + BLOCK - 1 < limit else col_minus_row < limit - b * BLOCK)
                    for b in range(SB_KEYS // BLOCK) if b * BLOCK - (SB_ROWS - 1) < limit]
            tiles.append(mine)
            zs.append(scores(stacked_queries(t0), start, len(mine) * BLOCK))
            vals.append(v_ref[0, pl.ds(start, len(mine) * BLOCK), :])
        terms = [dict() for _ in units]
        for b in reversed(range(SB_KEYS // BLOCK)):
            have = [(u, mask) for u, mine in enumerate(tiles) for tb, mask in mine if tb == b]
            parts = [log_pieces(zs[u][:, b * BLOCK:(b + 1) * BLOCK], mask) for u, mask in have]
            for (u, _), (log_beta, _), sums in zip(have, parts, tile_sums([p for _, p in parts])):
                terms[u][b] = (log_beta, sums)
        out = []
        for u, (t0, _, _) in enumerate(units):
            later, ws = None, []
            for b, mask in reversed(tiles[u]):
                log_beta, sums = terms[u][b]
                ws.insert(0, weights(log_beta, sums, later, mask))
                later = sums[:, BLOCK:] if later is None else later + sums[:, BLOCK:]
            pv = jnp.dot(jnp.concatenate(ws, axis=1), vals[u], preferred_element_type=F32)
            store(t0, pv)
            out.append((later, pv))
        return out

    def sweep_earlier(t0, start, total, pv):
        lhs = stacked_queries(t0)

        def cond(st):
            return (st[0] > 0) & (st[1] > 0)

        def body(st):
            end, _, carry, acc = st
            begin = pl.multiple_of(jnp.maximum(end - BLOCK, 0), SB_ROWS)
            mask = lane < end - begin
            log_beta, pieces = log_pieces(scores(lhs, begin, BLOCK), mask)
            sums, = tile_sums([pieces])
            w = weights(log_beta, sums, carry, mask)
            acc = acc + jnp.dot(w, v_ref[0, pl.ds(begin, BLOCK), :], preferred_element_type=F32)
            carry = carry + sums[:, BLOCK:]
            return begin, (jnp.max(carry) > cutoff).astype(jnp.int32), carry, acc

        st = lax.while_loop(cond, body, (jnp.int32(start), jnp.int32(1), total, pv))
        store(t0, st[3])

    def run_units(units):
        res = first_windows(units)
        check = [(u, r) for u, r in zip(units, res) if not (isinstance(u[1], int) and u[1] == 0)]
        if not check:
            return
        flags = [(jnp.max(total) > cutoff).astype(jnp.int32) for _, (total, _) in check]

        @pl.when(functools.reduce(jnp.maximum, flags) > 0)
        def _():
            for ((t0, start, _), (total, pv)), flag in zip(check, flags):
                pl.when(flag > 0)(functools.partial(sweep_earlier, t0, start, total, pv))

    nunit = q_ref.shape[1] // SB_ROWS
    reach = SB_KEYS - SB_ROWS
    nclamp = reach // SB_ROWS
    npeel = nclamp + (nunit - nclamp) % SB_GROUP
    run_units([(u * SB_ROWS, max(u * SB_ROWS - reach, 0), min(u * SB_ROWS, reach))
               for u in range(npeel)])

    def group(g, _):
        base = (npeel + g * SB_GROUP) * SB_ROWS
        t0s = [pl.multiple_of(base + m * SB_ROWS, SB_ROWS) for m in range(SB_GROUP)]
        run_units([(t0, pl.multiple_of(t0 - reach, SB_ROWS), reach) for t0 in t0s])
        return 0

    lax.fori_loop(0, (nunit - npeel) // SB_GROUP, group, 0)


def _attn_b(qkv):
    b, s, n = qkv.shape
    npair = n // 3 // LANES
    assert s % SB_ROWS == 0 and s >= SB_KEYS
    key = jnp.arange(BLOCK)
    later = (key[:, None] > key[None, :]).astype(BF16)
    uo = jnp.concatenate([later, jnp.ones((BLOCK, BLOCK), BF16)], axis=1)
    uo = jnp.concatenate([uo, uo], axis=0)
    blk = lambda off: pl.BlockSpec((1, s, LANES), lambda bi, hp: (bi, 0, off + hp))
    return pl.pallas_call(
        _attn_b_kernel,
        out_shape=jax.ShapeDtypeStruct((b, s, n // 3), BF16),
        grid=(b, npair),
        in_specs=[blk(0), blk(npair), blk(2 * npair), _resident(uo.shape)],
        out_specs=blk(0),
        compiler_params=_params(("parallel", "parallel")),
        name="attn_b",
    )(qkv, qkv, qkv, uo)


def _post_steps(o, x_ref, mod_ref, g_ref, wo_ref, wg_ref, wu_ref, wd_ref, out_ref, acc_ref):
    y = jnp.dot(o, wo_ref[...], preferred_element_type=F32)
    yield
    x1 = x_ref[0] + mod_ref[0, 2:3, :] * y
    h = _rms_mod(x1, g_ref[...], mod_ref[0, 4:5, :], mod_ref[0, 3:4, :]).astype(BF16)
    for f in range(wg_ref.shape[1] // FF_CHUNK):
        cols = slice(f * FF_CHUNK, (f + 1) * FF_CHUNK)
        g = jnp.dot(h, wg_ref[:, cols], preferred_element_type=F32)
        yield
        u = jnp.dot(h, wu_ref[:, cols], preferred_element_type=F32)
        yield
        act = ((g * jax.nn.sigmoid(g)) * u).astype(BF16)
        down = jnp.dot(act, wd_ref[cols, :], preferred_element_type=F32)
        if f == 0:
            acc_ref[...] = down
        else:
            acc_ref[...] += down
        yield
    out_ref[0] = x1 + mod_ref[0, 5:6, :] * acc_ref[...]


def _post_kernel(o_ref, *refs):
    for _ in _post_steps(o_ref[0], *refs):
        pass


def _layer_a_kernel(tiles_per_seq, sink_ref, q_ref, kc_ref, kp_ref, vc_ref, vp_ref, x_ref, mod_ref,
                    g_ref, wo_ref, wg_ref, wu_ref, wd_ref, out_ref, o_ref, acc_ref):
    t = pl.program_id(0)

    @pl.when(t == 0)
    def _():
        o_ref[...] = jnp.zeros_like(o_ref)

    post = _post_steps(o_ref[...], x_ref, mod_ref, g_ref, wo_ref, wg_ref, wu_ref, wd_ref, out_ref,
                       acc_ref)
    next(post)

    def store(jb, p, tile):
        o_ref[jb * BLOCK:(jb + 1) * BLOCK, p * LANES:(p + 1) * LANES] = tile

    last = pl.num_programs(0) - 2
    first = jnp.minimum(t, last) % tiles_per_seq == 0
    attn = _attn_a_steps(first, sink_ref, q_ref, kc_ref, kp_ref, vc_ref, vp_ref, store)
    done = object()
    running = True
    while running:
        running = next(attn, done) is not done
        for _ in range(LAYER_A_RATIO):
            running |= next(post, done) is not done


def _layer_a(sinks, q, k, v, x, wo, mod, gain2, wg, wu, wd):
    b, s, d = x.shape
    nq = q.shape[-1]
    tm = ROW_TILE
    assert tm == ATTN_A_BLOCKS * BLOCK
    n = s // tm
    ntile = b * n
    attn_tile = lambda t: jnp.minimum(t, ntile - 1)
    post_tile = lambda t: jnp.maximum(t - 1, 0)
    cur = lambda t: (attn_tile(t) // n, attn_tile(t) % n, 0)
    prev = lambda t: (attn_tile(t) // n, jnp.maximum((attn_tile(t) % n) * ATTN_A_BLOCKS - 1, 0), 0)
    post = lambda t: (post_tile(t) // n, post_tile(t) % n, 0)
    return pl.pallas_call(
        functools.partial(_layer_a_kernel, n),
        out_shape=jax.ShapeDtypeStruct((b, s, d), F32),
        grid=(ntile + 1,),
        in_specs=[
            pl.BlockSpec(memory_space=pltpu.SMEM),
            pl.BlockSpec((1, tm, nq), cur),
            pl.BlockSpec((1, tm, LANES), cur),
            pl.BlockSpec((1, BLOCK, LANES), prev),
            pl.BlockSpec((1, tm, LANES), cur),
            pl.BlockSpec((1, BLOCK, LANES), prev),
            pl.BlockSpec((1, tm, d), post),
            pl.BlockSpec((1, 6, d), lambda t: (post_tile(t) // n, 0, 0)),
            _resident((1, d)),
            _resident(wo.shape), _resident(wg.shape), _resident(wu.shape), _resident(wd.shape),
        ],
        out_specs=pl.BlockSpec((1, tm, d), post),
        scratch_shapes=[pltpu.VMEM((tm, nq), BF16), pltpu.VMEM((tm, d), F32)],
        compiler_params=_params(("arbitrary",)),
        name="layer_a",
    )(sinks, q, k, k, v, v, x, mod, gain2, wo, wg, wu, wd)


def _post(o, x, wo, mod, gain2, wg, wu, wd):
    b, s, d = x.shape
    tm = LAYER_B_ROWS
    row = lambda n: pl.BlockSpec((1, tm, n), lambda bi, i: (bi, i, 0))
    return pl.pallas_call(
        _post_kernel,
        out_shape=jax.ShapeDtypeStruct((b, s, d), F32),
        grid=(b, s // tm),
        in_specs=[
            row(o.shape[-1]), row(d),
            pl.BlockSpec((1, 6, d), lambda bi, i: (bi, 0, 0)),
            _resident((1, d)),
            _resident(wo.shape), _resident(wg.shape), _resident(wu.shape), _resident(wd.shape),
        ],
        out_specs=row(d),
        scratch_shapes=[pltpu.VMEM((tm, d), F32)],
        compiler_params=_params(("parallel", "parallel")),
        name="post",
    )(o, x, mod, gain2, wo, wg, wu, wd)


def _pair_perm():
    cols = []
    for p in range(GROUP_A):
        for h in (p, p + GROUP_A):
            cols.extend(range(h * HEAD_DIM, (h + 1) * HEAD_DIM))
    return jnp.asarray(cols, jnp.int32)


def kernel(x, c, positions, ada_w, ada_b, norm1_g, norm2_g, wqkv_a, q_norm_a, k_norm_a, sinks_a,
           wo_a, wqkv_b, wo_b, w_gate, w_up, w_down):
    b, s, d = x.shape
    depth = ada_w.shape[0]
    assert w_gate.shape[-1] % FF_CHUNK == 0
    scale = HEAD_DIM ** -0.5

    mod = _adaln(c, ada_w, ada_b).reshape(depth, b, 6, d)

    perm = _pair_perm()
    nqa = N_Q_A * HEAD_DIM
    lane = jnp.arange(LANES)
    gsum = (lane[:, None] // HEAD_DIM == lane[None, :] // HEAD_DIM).astype(BF16)

    for i in range(depth):
        j = i // 2
        post_args = (mod[i], norm2_g[i].reshape(1, d), w_gate[i].astype(BF16), w_up[i].astype(BF16),
                     w_down[i].astype(BF16))
        if i % 2 == 0:
            w = wqkv_a[j]
            w = jnp.concatenate([w[:, :nqa][:, perm], w[:, nqa:]], axis=1).astype(BF16)
            gains = jnp.stack([jnp.tile(q_norm_a[j] * scale, 2), jnp.tile(k_norm_a[j], 2)])
            q, k, v = _proj_a(x, mod[i], norm1_g[i].reshape(1, d), w, gains, gsum, positions)
            x = _layer_a(sinks_a[j], q, k, v, x, wo_a[j][perm, :].astype(BF16), *post_args)
        else:
            w = wqkv_b[j]
            nqb = N_H_B * HEAD_DIM
            w = jnp.concatenate([w[:, :nqb] * (scale * LOG2E), w[:, nqb:]], axis=1).astype(BF16)
            o = _attn_b(_proj_b(x, mod[i], norm1_g[i].reshape(1, d), w))
            x = _post(o, x, wo_b[j].astype(BF16), *post_args)
    return x
```
